```python
import math
import jax
import jax.numpy as jnp
from jax import lax
import numpy as np


D_MODEL = 1024
BATCH = 8
SEQ = 2048
DEPTH = 4

GRID_W = 64
NORM_EPS = 1e-6
LN_EPS = 1e-5

SSD_HEADS = 8
SSD_HEAD_DIM = 64
SSD_INNER = SSD_HEADS * SSD_HEAD_DIM
SSD_GROUPS = 2
SSD_STATE = 128
SSD_CONV_K = 5
SSD_CHUNK = 128
SSD_XBC = SSD_INNER + 2 * SSD_GROUPS * SSD_STATE

ATT_Q_HEADS = 8
ATT_KV_HEADS = 2
ATT_HEAD_DIM = 64
ATT_INNER = ATT_Q_HEADS * ATT_HEAD_DIM
ATT_BLOCK = 128
ROPE_THETA = 10000.0

POOL_WINDOWS = (2, 4, 8, 16)
POOL_GROUP = 128
POOL_INNER = POOL_GROUP * len(POOL_WINDOWS)

HGRN_HEADS = 4
HGRN_HEAD_DIM = 128
HGRN_INNER = HGRN_HEADS * HGRN_HEAD_DIM
HGRN_CHUNK = 64

N_EXPERTS = 16
EXPERT_FF = 2048
CAPACITY_FACTOR = 2

DEEPNORM_ALPHA = (2 * DEPTH) ** 0.25
DEEPNORM_BETA = (8 * DEPTH) ** -0.25

N_EVEN = (DEPTH + 1) // 2
N_ODD = DEPTH // 2

AB_WIDTHS = (SSD_INNER, SSD_XBC, 2 * SSD_HEADS, ATT_INNER, ATT_KV_HEADS * ATT_HEAD_DIM, ATT_KV_HEADS * ATT_HEAD_DIM)
CD_WIDTHS = (POOL_INNER, HGRN_INNER, HGRN_INNER, HGRN_INNER, HGRN_INNER, HGRN_INNER)
IN_AB = sum(AB_WIDTHS)
IN_CD = sum(CD_WIDTHS)
OUT_AB = SSD_INNER + ATT_INNER
OUT_CD = POOL_INNER + HGRN_INNER

kernel_name = 'hybrid_ssd_gqa_pool_hgrn2_ecmoe'


def split_cols(t, widths):
    offs = []
    acc = 0
    for w in widths[:-1]:
        acc += w
        offs.append(acc)
    return jnp.split(t, offs, axis=-1)


def rms_norm(x, w):
    xf = x.astype(jnp.float32)
    y = xf * lax.rsqrt(jnp.mean(xf * xf, axis=-1, keepdims=True) + NORM_EPS)
    return (y * w.astype(jnp.float32)).astype(x.dtype)


def layer_norm(x, g, b):
    xf = x.astype(jnp.float32)
    mu = jnp.mean(xf, axis=-1, keepdims=True)
    xc = xf - mu
    var = jnp.mean(xc * xc, axis=-1, keepdims=True)
    y = xc * lax.rsqrt(var + LN_EPS) * g.astype(jnp.float32) + b.astype(jnp.float32)
    return y.astype(x.dtype)


def flip_seq(t):
    return jnp.flip(t, axis=1)


def centred_depthwise_conv(u, w, b):
    k, c = w.shape
    pad = (k - 1) // 2
    y = lax.conv_general_dilated(u, w[:, None, :].astype(u.dtype), window_strides=(1,),
                                 padding=[(pad, k - 1 - pad)],
                                 dimension_numbers=('NWC', 'WIO', 'NWC'),
                                 feature_group_count=c)
    return y + b


def ssd_chunked(x, dt, a_neg, bm, cm):
    bsz, l, h, p = x.shape
    g, n = bm.shape[2], bm.shape[3]
    e = h // g
    q = SSD_CHUNK
    c = l // q
    f32 = jnp.float32
    la = (dt.astype(f32) * a_neg.astype(f32)).reshape(bsz, c, q, g, e)
    xs = (x.astype(f32) * dt.astype(f32)[..., None]).reshape(bsz, c, q, g, e, p)
    bs = bm.astype(f32).reshape(bsz, c, q, g, n)
    cs = cm.astype(f32).reshape(bsz, c, q, g, n)
    acum = jnp.cumsum(la, axis=2)
    lower = jnp.tril(jnp.ones((q, q), dtype=bool))[None, None, :, :, None, None]
    seg = acum[:, :, :, None] - acum[:, :, None, :]
    lmat = jnp.exp(jnp.where(lower, seg, -jnp.inf))
    cb = jnp.einsum('bcign,bcjgn->bcijg', cs, bs)
    y_diag = jnp.einsum('bcijge,bcjgep->bcigep', cb[..., None] * lmat, xs)
    decay_to_end = jnp.exp(acum[:, :, -1:] - acum)
    states = jnp.einsum('bcjgn,bcjge,bcjgep->bcgepn', bs, decay_to_end, xs)
    a_tot = acum[:, :, -1]
    t_inc = jnp.cumsum(a_tot, axis=1)
    t_exc = t_inc - a_tot
    before = jnp.tril(jnp.ones((c, c), dtype=bool), k=-1)[None, :, :, None, None]
    chunk_decay = jnp.exp(jnp.where(before, t_exc[:, :, None] - t_inc[:, None, :], -jnp.inf))
    s_in = jnp.einsum('bzyge,bygepn->bzgepn', chunk_decay, states)
    y_off = jnp.einsum('bcign,bcgepn,bcige->bcigep', cs, s_in, jnp.exp(acum))
    return (y_diag + y_off).reshape(bsz, l, h, p).astype(x.dtype)


def ssd_mixer(z, xbc, dt_raw, conv_w, conv_b, dt_bias, a_log, d_skip, norm_w):
    bsz, l, _ = xbc.shape
    xbc = jax.nn.silu(centred_depthwise_conv(xbc, conv_w, conv_b))
    xs, bm, cm = jnp.split(xbc, [SSD_INNER, SSD_INNER + SSD_GROUPS * SSD_STATE], axis=-1)
    xs = xs.reshape(bsz, l, SSD_HEADS, SSD_HEAD_DIM)
    bm = bm.reshape(bsz, l, SSD_GROUPS, SSD_STATE)
    cm = cm.reshape(bsz, l, SSD_GROUPS, SSD_STATE)
    dt = jax.nn.softplus((dt_raw + dt_bias).astype(jnp.float32))
    a_neg = -jnp.exp(a_log.astype(jnp.float32))
    y_f = ssd_chunked(xs, dt[:, :, 0], a_neg[0], bm, cm)
    y_b = flip_seq(ssd_chunked(flip_seq(xs), flip_seq(dt[:, :, 1]), a_neg[1], flip_seq(bm), flip_seq(cm)))
    y = (y_f + y_b + xs * d_skip[:, None]).reshape(bsz, l, SSD_INNER)
    return rms_norm(y * jax.nn.silu(z), norm_w)


def axial_rope_tables(seq):
    rows = seq // GRID_W
    row = jnp.repeat(jnp.arange(rows), GRID_W).astype(jnp.float32)
    col = jnp.tile(jnp.arange(GRID_W), rows).astype(jnp.float32)
    axis_dims = ATT_HEAD_DIM // 2
    freqs = ROPE_THETA ** (-jnp.arange(0, axis_dims, 2, dtype=jnp.float32) / axis_dims)
    ang = jnp.concatenate([row[:, None] * freqs, col[:, None] * freqs], axis=-1)
    return jnp.cos(ang), jnp.sin(ang)


def apply_rope(x, cos, sin):
    x2 = x.astype(jnp.float32).reshape(x.shape[:-1] + (x.shape[-1] // 2, 2))
    xe, xo = x2[..., 0], x2[..., 1]
    c = cos[None, :, None, :]
    s = sin[None, :, None, :]
    out = jnp.stack([xe * c - xo * s, xe * s + xo * c], axis=-1)
    return out.reshape(x.shape).astype(x.dtype)


def blocked_gqa(q, k, v):
    bsz, l, hq, d = q.shape
    hkv = k.shape[2]
    rep = hq // hkv
    nb = l // ATT_BLOCK
    scale = d ** -0.5
    qb = q.reshape(bsz, nb, ATT_BLOCK, hkv, rep, d).transpose(1, 0, 2, 3, 4, 5)

    def one_block(qblk):
        s = jnp.einsum('bqgrd,bkgd->bgrqk', qblk, k).astype(jnp.float32) * scale
        p = jax.nn.softmax(s, axis=-1).astype(v.dtype)
        return jnp.einsum('bgrqk,bkgd->bqgrd', p, v)

    o = lax.map(one_block, qb)
    return o.transpose(1, 0, 2, 3, 4, 5).reshape(bsz, l, hq * d)


def ab_mixer(x, w_in, conv_w, conv_b, dt_bias, a_log, d_skip, norm_w, q_norm_w, k_norm_w, w_out, cos, sin):
    bsz, l, _ = x.shape
    proj = jnp.einsum('bld,dc->blc', x, w_in)
    z, xbc, dt_raw, q, k, v = split_cols(proj, AB_WIDTHS)
    y_a = ssd_mixer(z, xbc, dt_raw.reshape(bsz, l, 2, SSD_HEADS), conv_w, conv_b, dt_bias, a_log, d_skip, norm_w)
    q = apply_rope(rms_norm(q.reshape(bsz, l, ATT_Q_HEADS, ATT_HEAD_DIM), q_norm_w), cos, sin)
    k = apply_rope(rms_norm(k.reshape(bsz, l, ATT_KV_HEADS, ATT_HEAD_DIM), k_norm_w), cos, sin)
    v = v.reshape(bsz, l, ATT_KV_HEADS, ATT_HEAD_DIM)
    y_b = blocked_gqa(q, k, v)
    return jnp.einsum('blc,cd->bld', jnp.concatenate([y_a, y_b], axis=-1), w_out)


def multiscale_pool(u, pool_w, pool_scale):
    bsz, l, _ = u.shape
    uf = u.astype(jnp.float32)
    csum = jnp.concatenate([jnp.zeros((bsz, 1, POOL_INNER), jnp.float32), jnp.cumsum(uf, axis=1)], axis=1)
    t = jnp.arange(l)
    outs = []
    for gi, w in enumerate(POOL_WINDOWS):
        lo = jnp.clip(t - w // 2, 0, l)
        hi = jnp.clip(t - w // 2 + w, 0, l)
        cg = csum[:, :, gi * POOL_GROUP:(gi + 1) * POOL_GROUP]
        mean = (cg[:, hi] - cg[:, lo]) / (hi - lo).astype(jnp.float32)[None, :, None]
        outs.append(mean - uf[:, :, gi * POOL_GROUP:(gi + 1) * POOL_GROUP])
    pooled = jnp.stack(outs, axis=2).astype(u.dtype)
    mixed = jnp.einsum('blgc,gcd->blgd', pooled, pool_w).reshape(bsz, l, POOL_INNER)
    return mixed * pool_scale


def hgrn2_scan(q, logf, k, v):
    bsz, l, h, dk = q.shape
    dv = v.shape[-1]
    cq = HGRN_CHUNK
    nc = l // cq

    def to_chunks(t):
        return t.reshape(bsz, nc, cq, h, t.shape[-1]).transpose(1, 0, 3, 2, 4)

    lower = jnp.tril(jnp.ones((cq, cq), dtype=bool))[:, :, None]

    def step(state, inp):
        qc, lfc, kc, vc = inp
        bcum = jnp.cumsum(lfc, axis=2)
        diff = bcum[:, :, :, None, :] - bcum[:, :, None, :, :]
        decay = jnp.exp(jnp.where(lower, diff, -jnp.inf))
        scores = jnp.einsum('bhic,bhjc,bhijc->bhij', qc, kc, decay)
        o = jnp.einsum('bhij,bhjv->bhiv', scores, vc) + jnp.einsum('bhic,bhcv->bhiv', qc * jnp.exp(bcum), state)
        last = bcum[:, :, -1]
        k_to_end = kc * jnp.exp(last[:, :, None, :] - bcum)
        state = jnp.exp(last)[..., None] * state + jnp.einsum('bhjc,bhjv->bhcv', k_to_end, vc)
        return state, o

    s0 = jnp.zeros((bsz, h, dk, dv), jnp.float32)
    _, o = lax.scan(step, s0, (to_chunks(q), to_chunks(logf), to_chunks(k), to_chunks(v)))
    return o.transpose(1, 0, 3, 2, 4).reshape(bsz, l, h, dv)


def hgrn2_mixer(q, f_fwd_raw, f_bwd_raw, i_in, g, lb, norm_w):
    bsz, l, _ = q.shape
    heads = lambda t: t.astype(jnp.float32).reshape(bsz, l, HGRN_HEADS, HGRN_HEAD_DIM)
    lbh = lb.reshape(HGRN_HEADS, HGRN_HEAD_DIM)

    def gate(raw):
        raw = heads(raw)
        logf = jnp.logaddexp(jnp.log(lbh), jnp.log1p(-lbh) + jax.nn.log_sigmoid(raw))
        return logf, (1.0 - lbh) * jax.nn.sigmoid(-raw)

    qh, vh = heads(q), heads(i_in)
    lf_f, k_f = gate(f_fwd_raw)
    lf_b, k_b = gate(f_bwd_raw)
    o_f = hgrn2_scan(qh, lf_f, k_f, vh)
    o_b = flip_seq(hgrn2_scan(flip_seq(qh), flip_seq(lf_b), flip_seq(k_b), flip_seq(vh)))
    o = rms_norm(o_f + o_b, norm_w.reshape(HGRN_HEADS, HGRN_HEAD_DIM)).reshape(bsz, l, HGRN_INNER)
    return (o * jax.nn.sigmoid(g.astype(jnp.float32))).astype(q.dtype)


def cd_mixer(x, w_in, pool_w, pool_scale, lb, norm_w, w_out):
    proj = jnp.einsum('bld,dc->blc', x, w_in)
    u_pool, q, f_f, f_b, i_in, g = split_cols(proj, CD_WIDTHS)
    y_c = multiscale_pool(u_pool, pool_w, pool_scale)
    y_d = hgrn2_mixer(q, f_f, f_b, i_in, g, lb, norm_w)
    return jnp.einsum('blc,cd->bld', jnp.concatenate([y_c, y_d], axis=-1), w_out)


def expert_choice_moe(x, router_w, w1, w3, w2):
    bsz, l, d = x.shape
    cap = CAPACITY_FACTOR * l // N_EXPERTS
    logits = jnp.einsum('bld,de->ble', x, router_w).astype(jnp.float32)
    aff = jax.nn.softmax(logits, axis=-1)
    gate, idx = lax.top_k(aff.transpose(0, 2, 1), cap)
    xs = jax.vmap(lambda xb, ib: xb[ib])(x, idx)
    hdn = jax.nn.silu(jnp.einsum('becd,edf->becf', xs, w1)) * jnp.einsum('becd,edf->becf', xs, w3)
    y = jnp.einsum('becf,efd->becd', hdn, w2) * gate[..., None].astype(x.dtype)
    return jax.vmap(lambda yb, ib: jnp.zeros((l, d), yb.dtype).at[ib.reshape(-1)].add(yb.reshape(-1, d)))(y, idx)


def setup_inputs(seed: int = 0) -> dict:
    key = jax.random.key(seed)
    ks = jax.random.split(key, 26)
    f32 = jnp.float32

    def nrm(k, shape, scale):
        return jax.random.normal(k, shape, f32) * scale

    ne, no, nl = N_EVEN, N_ODD, DEPTH
    x = nrm(ks[0], (BATCH, SEQ, D_MODEL), 1.0)
    w_in_ab = nrm(ks[1], (ne, D_MODEL, IN_AB), D_MODEL ** -0.5)
    ssm_conv_w = nrm(ks[2], (ne, SSD_CONV_K, SSD_XBC), SSD_CONV_K ** -0.5)
    ssm_conv_b = nrm(ks[3], (ne, SSD_XBC), 0.02)
    dt0 = jnp.exp(jax.random.uniform(ks[4], (ne, 2, SSD_HEADS), f32, minval=math.log(1e-3), maxval=math.log(1e-1)))
    ssm_dt_bias = dt0 + jnp.log(-jnp.expm1(-dt0))
    ssm_a_log = jnp.log(jax.random.uniform(ks[5], (ne, 2, SSD_HEADS), f32, minval=1.0, maxval=16.0))
    ssm_d = 1.0 + nrm(ks[6], (ne, SSD_HEADS), 0.02)
    ssm_norm_w = 1.0 + nrm(ks[7], (ne, SSD_INNER), 0.02)
    attn_q_norm = 1.0 + nrm(ks[8], (ne, ATT_HEAD_DIM), 0.02)
    attn_k_norm = 1.0 + nrm(ks[9], (ne, ATT_HEAD_DIM), 0.02)
    w_out_ab = nrm(ks[10], (ne, OUT_AB, D_MODEL), OUT_AB ** -0.5 * DEEPNORM_BETA)
    w_in_cd = nrm(ks[11], (no, D_MODEL, IN_CD), D_MODEL ** -0.5)
    pool_w = nrm(ks[12], (no, len(POOL_WINDOWS), POOL_GROUP, POOL_GROUP), POOL_GROUP ** -0.5)
    pool_scale = 1.0 + nrm(ks[13], (no, POOL_INNER), 0.02)
    hgrn_lb_logits = nrm(ks[14], (DEPTH, HGRN_INNER), 0.1)
    hgrn_norm_w = 1.0 + nrm(ks[15], (no, HGRN_INNER), 0.02)
    w_out_cd = nrm(ks[16], (no, OUT_CD, D_MODEL), OUT_CD ** -0.5 * DEEPNORM_BETA)
    router_w = nrm(ks[17], (nl, D_MODEL, N_EXPERTS), D_MODEL ** -0.5)
    moe_w1 = nrm(ks[18], (nl, N_EXPERTS, D_MODEL, EXPERT_FF), D_MODEL ** -0.5)
    moe_w3 = nrm(ks[19], (nl, N_EXPERTS, D_MODEL, EXPERT_FF), D_MODEL ** -0.5)
    moe_w2 = nrm(ks[20], (nl, N_EXPERTS, EXPERT_FF, D_MODEL), EXPERT_FF ** -0.5 * DEEPNORM_BETA)
    ln1_g = 1.0 + nrm(ks[21], (nl, D_MODEL), 0.02)
    ln1_b = nrm(ks[22], (nl, D_MODEL), 0.02)
    ln2_g = 1.0 + nrm(ks[23], (nl, D_MODEL), 0.02)
    ln2_b = nrm(ks[24], (nl, D_MODEL), 0.02)
    return {'x': x, 'w_in_ab': w_in_ab, 'ssm_conv_w': ssm_conv_w, 'ssm_conv_b': ssm_conv_b,
            'ssm_dt_bias': ssm_dt_bias, 'ssm_a_log': ssm_a_log, 'ssm_d': ssm_d, 'ssm_norm_w': ssm_norm_w,
            'attn_q_norm': attn_q_norm, 'attn_k_norm': attn_k_norm, 'w_out_ab': w_out_ab,
            'w_in_cd': w_in_cd, 'pool_w': pool_w, 'pool_scale': pool_scale,
            'hgrn_lb_logits': hgrn_lb_logits, 'hgrn_norm_w': hgrn_norm_w, 'w_out_cd': w_out_cd,
            'router_w': router_w, 'moe_w1': moe_w1, 'moe_w3': moe_w3, 'moe_w2': moe_w2,
            'ln1_g': ln1_g, 'ln1_b': ln1_b, 'ln2_g': ln2_g, 'ln2_b': ln2_b}


def reference(x, w_in_ab, ssm_conv_w, ssm_conv_b, ssm_dt_bias, ssm_a_log, ssm_d, ssm_norm_w,
              attn_q_norm, attn_k_norm, w_out_ab, w_in_cd, pool_w, pool_scale,
              hgrn_lb_logits, hgrn_norm_w, w_out_cd, router_w, moe_w1, moe_w3, moe_w2,
              ln1_g, ln1_b, ln2_g, ln2_b):
    seq = x.shape[1]
    cos, sin = axial_rope_tables(seq)
    lb_all = jnp.cumsum(jax.nn.softmax(hgrn_lb_logits.astype(jnp.float32), axis=0), axis=0)
    lb_all = lb_all - lb_all[0]
    for layer in range(DEPTH):
        j = layer // 2
        if layer % 2 == 0:
            mix = ab_mixer(x, w_in_ab[j], ssm_conv_w[j], ssm_conv_b[j], ssm_dt_bias[j], ssm_a_log[j],
                           ssm_d[j], ssm_norm_w[j], attn_q_norm[j], attn_k_norm[j], w_out_ab[j], cos, sin)
        else:
            mix = cd_mixer(x, w_in_cd[j], pool_w[j], pool_scale[j], lb_all[layer], hgrn_norm_w[j], w_out_cd[j])
        x = layer_norm(DEEPNORM_ALPHA * x + mix, ln1_g[layer], ln1_b[layer])
        ffn = expert_choice_moe(x, router_w[layer], moe_w1[layer], moe_w3[layer], moe_w2[layer])
        x = layer_norm(DEEPNORM_ALPHA * x + ffn, ln2_g[layer], ln2_b[layer])
    return x
```

```python
import functools
import math

import jax
import jax.numpy as jnp
from jax import lax
from jax.experimental import pallas as pl
from jax.experimental.pallas import tpu as pltpu

D_MODEL = 1024
DEPTH = 4
GRID_W = 64
NORM_EPS = 1e-6
LN_EPS = 1e-5

SSD_HEADS = 8
SSD_HEAD_DIM = 64
SSD_INNER = 512
SSD_GROUPS = 2
SSD_STATE = 128
SSD_CONV_K = 5
SSD_CHUNK = 128
SSD_XBC = 1024

ATT_Q_HEADS = 8
ATT_KV_HEADS = 2
ATT_HEAD_DIM = 64
ATT_INNER = 512
ATT_BLOCK = 128
ROPE_THETA = 10000.0

POOL_WINDOWS = (2, 4, 8, 16)
POOL_GROUP = 128
POOL_INNER = 512

HGRN_HEADS = 4
HGRN_HEAD_DIM = 128
HGRN_INNER = 512
HGRN_CHUNK = 64

N_EXPERTS = 16
EXPERT_FF = 2048
CAPACITY_FACTOR = 2

DEEPNORM_ALPHA = (2 * DEPTH) ** 0.25

AB_WIDTHS = (SSD_INNER, SSD_XBC, 2 * SSD_HEADS, ATT_INNER, ATT_KV_HEADS * ATT_HEAD_DIM, ATT_KV_HEADS * ATT_HEAD_DIM)
CD_WIDTHS = (POOL_INNER, HGRN_INNER, HGRN_INNER, HGRN_INNER, HGRN_INNER, HGRN_INNER)

VMEM_LIMIT = 56 * 1024 * 1024


def _mm_kernel(x_ref, w_ref, o_ref):
    o_ref[...] = jnp.dot(x_ref[...].astype(jnp.bfloat16), w_ref[...],
                         preferred_element_type=jnp.float32)


def matmul(x, w_bf16, *, tm=512, tn=None):
    m, k = x.shape
    n = w_bf16.shape[1]
    if tn is None:
        tn = n
    return pl.pallas_call(
        _mm_kernel,
        grid=(m // tm, n // tn),
        in_specs=[pl.BlockSpec((tm, k), lambda i, j: (i, 0)),
                  pl.BlockSpec((k, tn), lambda i, j: (0, j))],
        out_specs=pl.BlockSpec((tm, tn), lambda i, j: (i, j)),
        out_shape=jax.ShapeDtypeStruct((m, n), jnp.float32),
        compiler_params=pltpu.CompilerParams(
            dimension_semantics=("parallel", "parallel"), vmem_limit_bytes=VMEM_LIMIT),
        name="matmul",
    )(x, w_bf16)


FFN_TF = 512


def _ffn_kernel(xs_ref, w1_ref, w3_ref, w2_ref, o_ref, acc_ref):
    f = pl.program_id(1)
    xs = xs_ref[0]
    w1 = w1_ref[0].astype(jnp.bfloat16)
    w3 = w3_ref[0].astype(jnp.bfloat16)
    w2 = w2_ref[0].astype(jnp.bfloat16)
    h1 = jnp.dot(xs, w1, preferred_element_type=jnp.float32)
    h3 = jnp.dot(xs, w3, preferred_element_type=jnp.float32)
    hdn = (h1 * jax.nn.sigmoid(h1) * h3).astype(jnp.bfloat16)
    part = jnp.dot(hdn, w2, preferred_element_type=jnp.float32)

    @pl.when(f == 0)
    def _():
        acc_ref[...] = part

    @pl.when(f > 0)
    def _():
        acc_ref[...] += part

    @pl.when(f == pl.num_programs(1) - 1)
    def _():
        o_ref[0] = acc_ref[...]


def expert_ffn(xs, w1, w3, w2):
    e, r, d = xs.shape
    ff = w1.shape[2]
    tf = FFN_TF
    return pl.pallas_call(
        _ffn_kernel,
        grid=(e, ff // tf),
        in_specs=[pl.BlockSpec((1, r, d), lambda i, j: (i, 0, 0)),
                  pl.BlockSpec((1, d, tf), lambda i, j: (i, 0, j)),
                  pl.BlockSpec((1, d, tf), lambda i, j: (i, 0, j)),
                  pl.BlockSpec((1, tf, d), lambda i, j: (i, j, 0))],
        out_specs=pl.BlockSpec((1, r, d), lambda i, j: (i, 0, 0)),
        out_shape=jax.ShapeDtypeStruct((e, r, d), jnp.float32),
        scratch_shapes=[pltpu.VMEM((r, d), jnp.float32)],
        compiler_params=pltpu.CompilerParams(
            dimension_semantics=("parallel", "arbitrary"), vmem_limit_bytes=VMEM_LIMIT),
        name="expert_ffn",
    )(xs, w1, w3, w2)


def split_cols(t, widths):
    offs = []
    acc = 0
    for w in widths[:-1]:
        acc += w
        offs.append(acc)
    return jnp.split(t, offs, axis=-1)


def rms_norm(x, w):
    y = x * lax.rsqrt(jnp.mean(x * x, axis=-1, keepdims=True) + NORM_EPS)
    return y * w


def layer_norm(x, g, b):
    mu = jnp.mean(x, axis=-1, keepdims=True)
    xc = x - mu
    var = jnp.mean(xc * xc, axis=-1, keepdims=True)
    return xc * lax.rsqrt(var + LN_EPS) * g + b


def flip_seq(t):
    return jnp.flip(t, axis=1)


def centred_depthwise_conv(u, w, b):
    k, c = w.shape
    pad = (k - 1) // 2
    y = lax.conv_general_dilated(u, w[:, None, :], window_strides=(1,),
                                 padding=[(pad, k - 1 - pad)],
                                 dimension_numbers=('NWC', 'WIO', 'NWC'),
                                 feature_group_count=c)
    return y + b


def ssd_chunked(x, dt, a_neg, bm, cm):
    bsz, l, h, p = x.shape
    g, n = bm.shape[2], bm.shape[3]
    e = h // g
    q = SSD_CHUNK
    c = l // q
    la = (dt * a_neg).reshape(bsz, c, q, g, e)
    xs = (x * dt[..., None]).reshape(bsz, c, q, g, e, p)
    bs = bm.reshape(bsz, c, q, g, n)
    cs = cm.reshape(bsz, c, q, g, n)
    acum = jnp.cumsum(la, axis=2)
    lower = jnp.tril(jnp.ones((q, q), dtype=bool))[None, None, :, :, None, None]
    seg = acum[:, :, :, None] - acum[:, :, None, :]
    lmat = jnp.exp(jnp.where(lower, seg, -jnp.inf))
    cb = jnp.einsum('bcign,bcjgn->bcijg', cs, bs)
    y_diag = jnp.einsum('bcijge,bcjgep->bcigep', cb[..., None] * lmat, xs)
    decay_to_end = jnp.exp(acum[:, :, -1:] - acum)
    states = jnp.einsum('bcjgn,bcjge,bcjgep->bcgepn', bs, decay_to_end, xs)
    a_tot = acum[:, :, -1]
    t_inc = jnp.cumsum(a_tot, axis=1)
    t_exc = t_inc - a_tot
    before = jnp.tril(jnp.ones((c, c), dtype=bool), k=-1)[None, :, :, None, None]
    chunk_decay = jnp.exp(jnp.where(before, t_exc[:, :, None] - t_inc[:, None, :], -jnp.inf))
    s_in = jnp.einsum('bzyge,bygepn->bzgepn', chunk_decay, states)
    y_off = jnp.einsum('bcign,bcgepn,bcige->bcigep', cs, s_in, jnp.exp(acum))
    return (y_diag + y_off).reshape(bsz, l, h, p)


def ssd_mixer(z, xbc, dt_raw, conv_w, conv_b, dt_bias, a_log, d_skip, norm_w):
    bsz, l, _ = xbc.shape
    xbc = jax.nn.silu(centred_depthwise_conv(xbc, conv_w, conv_b))
    xs, bm, cm = jnp.split(xbc, [SSD_INNER, SSD_INNER + SSD_GROUPS * SSD_STATE], axis=-1)
    xs = xs.reshape(bsz, l, SSD_HEADS, SSD_HEAD_DIM)
    bm = bm.reshape(bsz, l, SSD_GROUPS, SSD_STATE)
    cm = cm.reshape(bsz, l, SSD_GROUPS, SSD_STATE)
    dt = jax.nn.softplus(dt_raw + dt_bias)
    a_neg = -jnp.exp(a_log)
    y_f = ssd_chunked(xs, dt[:, :, 0], a_neg[0], bm, cm)
    y_b = flip_seq(ssd_chunked(flip_seq(xs), flip_seq(dt[:, :, 1]), a_neg[1], flip_seq(bm), flip_seq(cm)))
    y = (y_f + y_b + xs * d_skip[:, None]).reshape(bsz, l, SSD_INNER)
    return rms_norm(y * jax.nn.silu(z), norm_w)


def axial_rope_tables(seq):
    rows = seq // GRID_W
    row = jnp.repeat(jnp.arange(rows), GRID_W).astype(jnp.float32)
    col = jnp.tile(jnp.arange(GRID_W), rows).astype(jnp.float32)
    axis_dims = ATT_HEAD_DIM // 2
    freqs = ROPE_THETA ** (-jnp.arange(0, axis_dims, 2, dtype=jnp.float32) / axis_dims)
    ang = jnp.concatenate([row[:, None] * freqs, col[:, None] * freqs], axis=-1)
    return jnp.cos(ang), jnp.sin(ang)


def apply_rope(x, cos, sin):
    x2 = x.reshape(x.shape[:-1] + (x.shape[-1] // 2, 2))
    xe, xo = x2[..., 0], x2[..., 1]
    c = cos[None, :, None, :]
    s = sin[None, :, None, :]
    out = jnp.stack([xe * c - xo * s, xe * s + xo * c], axis=-1)
    return out.reshape(x.shape)


def blocked_gqa(q, k, v):
    bsz, l, hq, d = q.shape
    hkv = k.shape[2]
    rep = hq // hkv
    nb = l // ATT_BLOCK
    scale = d ** -0.5
    qb = q.reshape(bsz, nb, ATT_BLOCK, hkv, rep, d).transpose(1, 0, 2, 3, 4, 5)

    def one_block(qblk):
        s = jnp.einsum('bqgrd,bkgd->bgrqk', qblk, k) * scale
        p = jax.nn.softmax(s, axis=-1)
        return jnp.einsum('bgrqk,bkgd->bqgrd', p, v)

    o = lax.map(one_block, qb)
    return o.transpose(1, 0, 2, 3, 4, 5).reshape(bsz, l, hq * d)


def multiscale_pool(u, pool_w, pool_scale):
    bsz, l, _ = u.shape
    csum = jnp.concatenate([jnp.zeros((bsz, 1, POOL_INNER), jnp.float32), jnp.cumsum(u, axis=1)], axis=1)
    t = jnp.arange(l)
    outs = []
    for gi, w in enumerate(POOL_WINDOWS):
        lo = jnp.clip(t - w // 2, 0, l)
        hi = jnp.clip(t - w // 2 + w, 0, l)
        cg = csum[:, :, gi * POOL_GROUP:(gi + 1) * POOL_GROUP]
        mean = (cg[:, hi] - cg[:, lo]) / (hi - lo).astype(jnp.float32)[None, :, None]
        outs.append(mean - u[:, :, gi * POOL_GROUP:(gi + 1) * POOL_GROUP])
    pooled = jnp.stack(outs, axis=2)
    mixed = jnp.einsum('blgc,gcd->blgd', pooled, pool_w).reshape(bsz, l, POOL_INNER)
    return mixed * pool_scale


def hgrn2_scan(q, logf, k, v):
    bsz, l, h, dk = q.shape
    dv = v.shape[-1]
    cq = HGRN_CHUNK
    nc = l // cq

    def to_chunks(t):
        return t.reshape(bsz, nc, cq, h, t.shape[-1]).transpose(1, 0, 3, 2, 4)

    lower = jnp.tril(jnp.ones((cq, cq), dtype=bool))[:, :, None]

    def step(state, inp):
        qc, lfc, kc, vc = inp
        bcum = jnp.cumsum(lfc, axis=2)
        diff = bcum[:, :, :, None, :] - bcum[:, :, None, :, :]
        decay = jnp.exp(jnp.where(lower, diff, -jnp.inf))
        scores = jnp.einsum('bhic,bhjc,bhijc->bhij', qc, kc, decay)
        o = jnp.einsum('bhij,bhjv->bhiv', scores, vc) + jnp.einsum('bhic,bhcv->bhiv', qc * jnp.exp(bcum), state)
        last = bcum[:, :, -1]
        k_to_end = kc * jnp.exp(last[:, :, None, :] - bcum)
        state = jnp.exp(last)[..., None] * state + jnp.einsum('bhjc,bhjv->bhcv', k_to_end, vc)
        return state, o

    s0 = jnp.zeros((bsz, h, dk, dv), jnp.float32)
    _, o = lax.scan(step, s0, (to_chunks(q), to_chunks(logf), to_chunks(k), to_chunks(v)))
    return o.transpose(1, 0, 3, 2, 4).reshape(bsz, l, h, dv)


def hgrn2_mixer(q, f_fwd_raw, f_bwd_raw, i_in, g, lb, norm_w):
    bsz, l, _ = q.shape
    heads = lambda t: t.reshape(bsz, l, HGRN_HEADS, HGRN_HEAD_DIM)
    lbh = lb.reshape(HGRN_HEADS, HGRN_HEAD_DIM)

    def gate(raw):
        raw = heads(raw)
        logf = jnp.logaddexp(jnp.log(lbh), jnp.log1p(-lbh) + jax.nn.log_sigmoid(raw))
        return logf, (1.0 - lbh) * jax.nn.sigmoid(-raw)

    qh, vh = heads(q), heads(i_in)
    lf_f, k_f = gate(f_fwd_raw)
    lf_b, k_b = gate(f_bwd_raw)
    o_f = hgrn2_scan(qh, lf_f, k_f, vh)
    o_b = flip_seq(hgrn2_scan(flip_seq(qh), flip_seq(lf_b), flip_seq(k_b), flip_seq(vh)))
    o = rms_norm(o_f + o_b, norm_w.reshape(HGRN_HEADS, HGRN_HEAD_DIM)).reshape(bsz, l, HGRN_INNER)
    return o * jax.nn.sigmoid(g)


def expert_choice_moe(x, router_w, w1, w3, w2):
    bsz, l, d = x.shape
    cap = CAPACITY_FACTOR * l // N_EXPERTS
    logits = jnp.einsum('bld,de->ble', x, router_w, precision=lax.Precision.HIGHEST)
    aff = jax.nn.softmax(logits, axis=-1)
    gate, idx = lax.top_k(aff.transpose(0, 2, 1), cap)
    xs = jax.vmap(lambda xb, ib: xb[ib])(x, idx)
    xs = xs.transpose(1, 0, 2, 3).reshape(N_EXPERTS, bsz * cap, d).astype(jnp.bfloat16)
    y = expert_ffn(xs, w1, w3, w2)
    y = y.reshape(N_EXPERTS, bsz, cap, d).transpose(1, 0, 2, 3) * gate[..., None]
    return jax.vmap(lambda yb, ib: jnp.zeros((l, d), yb.dtype).at[ib.reshape(-1)].add(yb.reshape(-1, d)))(y, idx)


def kernel(x, w_in_ab, ssm_conv_w, ssm_conv_b, ssm_dt_bias, ssm_a_log, ssm_d, ssm_norm_w, attn_q_norm, attn_k_norm, w_out_ab, w_in_cd, pool_w, pool_scale, hgrn_lb_logits, hgrn_norm_w, w_out_cd, router_w, moe_w1, moe_w3, moe_w2, ln1_g, ln1_b, ln2_g, ln2_b):
    bsz, seq, d = x.shape
    m = bsz * seq
    cos, sin = axial_rope_tables(seq)
    lb_all = jnp.cumsum(jax.nn.softmax(hgrn_lb_logits, axis=0), axis=0)
    lb_all = lb_all - lb_all[0]
    for layer in range(DEPTH):
        j = layer // 2
        x2 = x.reshape(m, d)
        if layer % 2 == 0:
            w = w_in_ab[j]
            w_main = jnp.concatenate([w[:, :1536], w[:, 1552:]], axis=1).astype(jnp.bfloat16)
            proj = matmul(x2, w_main, tn=768)
            dt_raw = jnp.dot(x2, w[:, 1536:1552], precision=lax.Precision.HIGHEST)
            z = proj[:, :512].reshape(bsz, seq, -1)
            xbc = proj[:, 512:1536].reshape(bsz, seq, -1)
            q = proj[:, 1536:2048].reshape(bsz, seq, ATT_Q_HEADS, ATT_HEAD_DIM)
            k = proj[:, 2048:2176].reshape(bsz, seq, ATT_KV_HEADS, ATT_HEAD_DIM)
            v = proj[:, 2176:2304].reshape(bsz, seq, ATT_KV_HEADS, ATT_HEAD_DIM)
            y_a = ssd_mixer(z, xbc, dt_raw.reshape(bsz, seq, 2, SSD_HEADS), ssm_conv_w[j], ssm_conv_b[j],
                            ssm_dt_bias[j], ssm_a_log[j], ssm_d[j], ssm_norm_w[j])
            q = apply_rope(rms_norm(q, attn_q_norm[j]), cos, sin)
            k = apply_rope(rms_norm(k, attn_k_norm[j]), cos, sin)
            y_b = blocked_gqa(q, k, v)
            y = jnp.concatenate([y_a, y_b], axis=-1).reshape(m, -1)
            mix = matmul(y, w_out_ab[j].astype(jnp.bfloat16))
        else:
            proj = matmul(x2, w_in_cd[j].astype(jnp.bfloat16), tn=768).reshape(bsz, seq, -1)
            u_pool, q, f_f, f_b, i_in, g = split_cols(proj, CD_WIDTHS)
            y_c = multiscale_pool(u_pool, pool_w[j], pool_scale[j])
            y_d = hgrn2_mixer(q, f_f, f_b, i_in, g, lb_all[layer], hgrn_norm_w[j])
            y = jnp.concatenate([y_c, y_d], axis=-1).reshape(m, -1)
            mix = matmul(y, w_out_cd[j].astype(jnp.bfloat16))
        x = layer_norm(DEEPNORM_ALPHA * x + mix.reshape(bsz, seq, d), ln1_g[layer], ln1_b[layer])
        ffn = expert_choice_moe(x, router_w[layer], moe_w1[layer], moe_w3[layer], moe_w2[layer])
        x = layer_norm(DEEPNORM_ALPHA * x + ffn, ln2_g[layer], ln2_b[layer])
    return x
```

```python
import functools
import math

import jax
import jax.numpy as jnp
from jax import lax
from jax.experimental import pallas as pl
from jax.experimental.pallas import tpu as pltpu

D_MODEL = 1024
DEPTH = 4
GRID_W = 64
NORM_EPS = 1e-6
LN_EPS = 1e-5

SSD_HEADS = 8
SSD_HEAD_DIM = 64
SSD_INNER = 512
SSD_GROUPS = 2
SSD_STATE = 128
SSD_CONV_K = 5
SSD_CHUNK = 128
SSD_XBC = 1024

ATT_Q_HEADS = 8
ATT_KV_HEADS = 2
ATT_HEAD_DIM = 64
ATT_INNER = 512
ATT_BLOCK = 128
ROPE_THETA = 10000.0

POOL_WINDOWS = (2, 4, 8, 16)
POOL_GROUP = 128
POOL_INNER = 512

HGRN_HEADS = 4
HGRN_HEAD_DIM = 128
HGRN_INNER = 512
HGRN_CHUNK = 64

N_EXPERTS = 16
EXPERT_FF = 2048
CAPACITY_FACTOR = 2

DEEPNORM_ALPHA = (2 * DEPTH) ** 0.25

AB_WIDTHS = (SSD_INNER, SSD_XBC, 2 * SSD_HEADS, ATT_INNER, ATT_KV_HEADS * ATT_HEAD_DIM, ATT_KV_HEADS * ATT_HEAD_DIM)
CD_WIDTHS = (POOL_INNER, HGRN_INNER, HGRN_INNER, HGRN_INNER, HGRN_INNER, HGRN_INNER)

VMEM_LIMIT = 56 * 1024 * 1024


def _mm_kernel(x_ref, w_ref, o_ref):
    o_ref[...] = jnp.dot(x_ref[...].astype(jnp.bfloat16), w_ref[...],
                         preferred_element_type=jnp.float32)


def matmul(x, w_bf16, *, tm=512, tn=None):
    m, k = x.shape
    n = w_bf16.shape[1]
    if tn is None:
        tn = n
    return pl.pallas_call(
        _mm_kernel,
        grid=(m // tm, n // tn),
        in_specs=[pl.BlockSpec((tm, k), lambda i, j: (i, 0)),
                  pl.BlockSpec((k, tn), lambda i, j: (0, j))],
        out_specs=pl.BlockSpec((tm, tn), lambda i, j: (i, j)),
        out_shape=jax.ShapeDtypeStruct((m, n), jnp.float32),
        compiler_params=pltpu.CompilerParams(
            dimension_semantics=("parallel", "parallel"), vmem_limit_bytes=VMEM_LIMIT),
        name="matmul",
    )(x, w_bf16)


FFN_TF = 512


def _ffn_kernel(xs_ref, w1_ref, w3_ref, w2_ref, o_ref, acc_ref):
    f = pl.program_id(1)
    xs = xs_ref[0]
    w1 = w1_ref[0].astype(jnp.bfloat16)
    w3 = w3_ref[0].astype(jnp.bfloat16)
    w2 = w2_ref[0].astype(jnp.bfloat16)
    h1 = jnp.dot(xs, w1, preferred_element_type=jnp.float32)
    h3 = jnp.dot(xs, w3, preferred_element_type=jnp.float32)
    hdn = (h1 * jax.nn.sigmoid(h1) * h3).astype(jnp.bfloat16)
    part = jnp.dot(hdn, w2, preferred_element_type=jnp.float32)

    @pl.when(f == 0)
    def _():
        acc_ref[...] = part

    @pl.when(f > 0)
    def _():
        acc_ref[...] += part

    @pl.when(f == pl.num_programs(1) - 1)
    def _():
        o_ref[0] = acc_ref[...]


def expert_ffn(xs, w1, w3, w2):
    e, r, d = xs.shape
    ff = w1.shape[2]
    tf = FFN_TF
    return pl.pallas_call(
        _ffn_kernel,
        grid=(e, ff // tf),
        in_specs=[pl.BlockSpec((1, r, d), lambda i, j: (i, 0, 0)),
                  pl.BlockSpec((1, d, tf), lambda i, j: (i, 0, j)),
                  pl.BlockSpec((1, d, tf), lambda i, j: (i, 0, j)),
                  pl.BlockSpec((1, tf, d), lambda i, j: (i, j, 0))],
        out_specs=pl.BlockSpec((1, r, d), lambda i, j: (i, 0, 0)),
        out_shape=jax.ShapeDtypeStruct((e, r, d), jnp.float32),
        scratch_shapes=[pltpu.VMEM((r, d), jnp.float32)],
        compiler_params=pltpu.CompilerParams(
            dimension_semantics=("parallel", "arbitrary"), vmem_limit_bytes=VMEM_LIMIT),
        name="expert_ffn",
    )(xs, w1, w3, w2)


HG_T = 16


def _hgrn_gate(raw, log_lb, log1m_lb, one_m_lb):
    ls = jnp.minimum(raw, 0.0) - jnp.log1p(jnp.exp(-jnp.abs(raw)))
    bb = log1m_lb + ls
    mx = jnp.maximum(log_lb, bb)
    logf = mx + jnp.log1p(jnp.exp(-jnp.abs(log_lb - bb)))
    return logf, one_m_lb * jax.nn.sigmoid(-raw)


def _hgrn_block(q, k, lf, v, st_ref, row, backward):
    t = HG_T
    w = jnp.zeros_like(q)
    o = jnp.zeros_like(q)
    for d in range(t):
        sh = (t - d) % t if backward else d
        if d == 0:
            kd, vd, lfd = k, v, lf
        else:
            kd = pltpu.roll(k, sh, axis=0)
            vd = pltpu.roll(v, sh, axis=0)
            lfd = pltpu.roll(lf, sh, axis=0)
        valid = (row + d <= t - 1) if backward else (row >= d)
        p = jnp.where(valid, q * kd * jnp.exp(w), 0.0)
        o = o + jnp.sum(p, axis=-1, keepdims=True) * vd
        w = w + lfd
    c = lf
    s = 1
    while s < t:
        if backward:
            c = c + jnp.where(row + s <= t - 1, pltpu.roll(c, t - s, axis=0), 0.0)
        else:
            c = c + jnp.where(row >= s, pltpu.roll(c, s, axis=0), 0.0)
        s *= 2
    tot = c[0:1, :] if backward else c[t - 1:t, :]
    st = st_ref[...]
    qt = (q * jnp.exp(c)).astype(jnp.bfloat16)
    o = o + lax.dot_general(qt, st.astype(jnp.bfloat16), (((1,), (1,)), ((), ())),
                            preferred_element_type=jnp.float32)
    kh = (k * jnp.exp(tot - c)).astype(jnp.bfloat16)
    ut = lax.dot_general(v.astype(jnp.bfloat16), kh, (((0,), (0,)), ((), ())),
                         preferred_element_type=jnp.float32)
    st_ref[...] = st * jnp.exp(tot) + ut
    return o


def _hgrn_kernel(q_ref, ff_ref, fb_ref, i_ref, g_ref, lb_ref, nw_ref, o_ref,
                 lff_ref, kf_ref, lfb_ref, kb_ref, of_ref, ob_ref, stf_ref, stb_ref):
    lb = lb_ref[...]
    log_lb = jnp.log(lb)
    log1m_lb = jnp.log1p(-lb)
    one_m_lb = 1.0 - lb
    lf, kk = _hgrn_gate(ff_ref[...], log_lb, log1m_lb, one_m_lb)
    lff_ref[...] = lf
    kf_ref[...] = kk
    lf, kk = _hgrn_gate(fb_ref[...], log_lb, log1m_lb, one_m_lb)
    lfb_ref[...] = lf
    kb_ref[...] = kk
    stf_ref[...] = jnp.zeros_like(stf_ref)
    stb_ref[...] = jnp.zeros_like(stb_ref)
    nblk = q_ref.shape[0] // HG_T
    row = lax.broadcasted_iota(jnp.int32, (HG_T, HGRN_HEAD_DIM), 0)

    def body(n, carry):
        rf = pl.ds(pl.multiple_of(n * HG_T, HG_T), HG_T)
        rb = pl.ds(pl.multiple_of((nblk - 1 - n) * HG_T, HG_T), HG_T)
        of_ref[rf, :] = _hgrn_block(q_ref[rf, :], kf_ref[rf, :], lff_ref[rf, :], i_ref[rf, :], stf_ref, row, False)
        ob_ref[rb, :] = _hgrn_block(q_ref[rb, :], kb_ref[rb, :], lfb_ref[rb, :], i_ref[rb, :], stb_ref, row, True)
        return carry

    lax.fori_loop(0, nblk, body, 0)
    o = of_ref[...] + ob_ref[...]
    o = o * lax.rsqrt(jnp.mean(o * o, axis=-1, keepdims=True) + NORM_EPS) * nw_ref[...]
    o_ref[...] = o * jax.nn.sigmoid(g_ref[...])


def hgrn2(proj, lb, norm_w, bsz, seq):
    m = proj.shape[0]
    hd = HGRN_HEAD_DIM
    nh = HGRN_HEADS

    def col(base):
        return pl.BlockSpec((seq, hd), lambda b, h: (b, base * nh + h))

    vec = pl.BlockSpec((1, hd), lambda b, h: (0, h))
    f32 = jnp.float32
    return pl.pallas_call(
        _hgrn_kernel,
        grid=(bsz, nh),
        in_specs=[col(1), col(2), col(3), col(4), col(5), vec, vec],
        out_specs=pl.BlockSpec((seq, hd), lambda b, h: (b, h)),
        out_shape=jax.ShapeDtypeStruct((m, HGRN_INNER), f32),
        scratch_shapes=[pltpu.VMEM((seq, hd), f32) for _ in range(6)]
        + [pltpu.VMEM((hd, hd), f32) for _ in range(2)],
        compiler_params=pltpu.CompilerParams(
            dimension_semantics=("parallel", "parallel"), vmem_limit_bytes=VMEM_LIMIT),
        name="hgrn2",
    )(proj, proj, proj, proj, proj, lb.reshape(1, -1), norm_w.reshape(1, -1))


ATT_TQ = 256
LANES = 128


def _group_mean_sq(x, g_ref):
    x2 = x * x
    hi = x2.astype(jnp.bfloat16)
    lo = (x2 - hi.astype(jnp.float32)).astype(jnp.bfloat16)
    g = g_ref[...]
    return (jnp.dot(hi, g, preferred_element_type=jnp.float32)
            + jnp.dot(lo, g, preferred_element_type=jnp.float32))


def _norm_rope(x, g_ref, w, cos, sin_signed):
    n = x.shape[1]
    y = x * lax.rsqrt(_group_mean_sq(x, g_ref) + NORM_EPS) * w
    lane = lax.broadcasted_iota(jnp.int32, y.shape, 1)
    partner = jnp.where(lane % 2 == 0, pltpu.roll(y, n - 1, axis=1), pltpu.roll(y, 1, axis=1))
    return y * cos + partner * sin_signed


def _dup_halves(x):
    lane = lax.broadcasted_iota(jnp.int32, x.shape, 1)
    sw = pltpu.roll(x, ATT_HEAD_DIM, axis=1)
    lo = lane < ATT_HEAD_DIM
    return jnp.where(lo, x, sw), jnp.where(lo, sw, x)


def _gqa_kernel(q_ref, k_ref, v_ref, cq_ref, sq_ref, ck_ref, sk_ref, gq_ref, gk_ref, qw_ref, kw_ref,
                o_ref, kd_ref, vd_ref):
    @pl.when(pl.program_id(1) == 0)
    def _():
        kr = _norm_rope(k_ref[...], gk_ref, kw_ref[...], ck_ref[...], sk_ref[...])
        k0, k1 = _dup_halves(kr)
        kd_ref[0] = k0.astype(jnp.bfloat16)
        kd_ref[1] = k1.astype(jnp.bfloat16)
        v0, v1 = _dup_halves(v_ref[...])
        vd_ref[0] = v0.astype(jnp.bfloat16)
        vd_ref[1] = v1.astype(jnp.bfloat16)

    tq = q_ref.shape[0]
    reps = ATT_INNER // LANES
    cos = jnp.concatenate([cq_ref[...]] * reps, axis=1)
    sin = jnp.concatenate([sq_ref[...]] * reps, axis=1)
    qr = _norm_rope(q_ref[...], gq_ref, qw_ref[...], cos, sin) * (ATT_HEAD_DIM ** -0.5)
    lane = lax.broadcasted_iota(jnp.int32, (tq, LANES), 1)
    lo = lane < ATT_HEAD_DIM
    rep = ATT_Q_HEADS // ATT_KV_HEADS
    for pair in range(ATT_Q_HEADS // 2):
        grp = (2 * pair) // rep
        qp = qr[:, pair * LANES:(pair + 1) * LANES]
        q2 = jnp.concatenate([jnp.where(lo, qp, 0.0), jnp.where(lo, 0.0, qp)], axis=0).astype(jnp.bfloat16)
        s = lax.dot_general(q2, kd_ref[grp], (((1,), (1,)), ((), ())), preferred_element_type=jnp.float32)
        s = s - jnp.max(s, axis=-1, keepdims=True)
        p = jnp.exp(s)
        l = jnp.sum(p, axis=-1, keepdims=True)
        o2 = jnp.dot(p.astype(jnp.bfloat16), vd_ref[grp], preferred_element_type=jnp.float32) / l
        o_ref[:, pair * LANES:(pair + 1) * LANES] = jnp.where(lo, o2[:tq], o2[tq:])


def gqa(proj, cos_t, sin_t, gq, q_w, k_w, bsz, seq):
    m = proj.shape[0]
    tq = ATT_TQ
    nq = seq // tq
    kvw = ATT_KV_HEADS * ATT_HEAD_DIM
    full = lambda shape: pl.BlockSpec(shape, lambda b, i: (0, 0))
    return pl.pallas_call(
        _gqa_kernel,
        grid=(bsz, nq),
        in_specs=[pl.BlockSpec((tq, ATT_INNER), lambda b, i: (b * nq + i, 1536 // ATT_INNER)),
                  pl.BlockSpec((seq, kvw), lambda b, i: (b, 2048 // kvw)),
                  pl.BlockSpec((seq, kvw), lambda b, i: (b, 2176 // kvw)),
                  pl.BlockSpec((tq, LANES), lambda b, i: (i, 0)),
                  pl.BlockSpec((tq, LANES), lambda b, i: (i, 0)),
                  full((seq, LANES)), full((seq, LANES)),
                  full((ATT_INNER, ATT_INNER)), full((LANES, LANES)),
                  full((1, ATT_INNER)), full((1, LANES))],
        out_specs=pl.BlockSpec((tq, ATT_INNER), lambda b, i: (b * nq + i, 0)),
        out_shape=jax.ShapeDtypeStruct((m, ATT_INNER), jnp.float32),
        scratch_shapes=[pltpu.VMEM((ATT_KV_HEADS, seq, LANES), jnp.bfloat16),
                        pltpu.VMEM((ATT_KV_HEADS, seq, LANES), jnp.bfloat16)],
        compiler_params=pltpu.CompilerParams(
            dimension_semantics=("parallel", "arbitrary"), vmem_limit_bytes=VMEM_LIMIT),
        name="gqa",
    )(proj, proj, proj, cos_t, sin_t, cos_t, sin_t, gq, gq[:LANES, :LANES],
      jnp.tile(q_w, ATT_Q_HEADS).reshape(1, -1), jnp.tile(k_w, ATT_KV_HEADS).reshape(1, -1))


def rope_lane_tables(seq):
    cos, sin = axial_rope_tables(seq)
    cos2 = jnp.repeat(cos, 2, axis=1)
    sin2 = jnp.repeat(sin, 2, axis=1) * jnp.tile(jnp.array([-1.0, 1.0], jnp.float32), ATT_HEAD_DIM // 2)
    return jnp.tile(cos2, (1, 2)), jnp.tile(sin2, (1, 2))


def head_mean_matrix():
    idx = jnp.arange(ATT_INNER) // ATT_HEAD_DIM
    return jnp.where(idx[:, None] == idx[None, :], 1.0 / ATT_HEAD_DIM, 0.0).astype(jnp.bfloat16)


def split_cols(t, widths):
    offs = []
    acc = 0
    for w in widths[:-1]:
        acc += w
        offs.append(acc)
    return jnp.split(t, offs, axis=-1)


def rms_norm(x, w):
    y = x * lax.rsqrt(jnp.mean(x * x, axis=-1, keepdims=True) + NORM_EPS)
    return y * w


def layer_norm(x, g, b):
    mu = jnp.mean(x, axis=-1, keepdims=True)
    xc = x - mu
    var = jnp.mean(xc * xc, axis=-1, keepdims=True)
    return xc * lax.rsqrt(var + LN_EPS) * g + b


def flip_seq(t):
    return jnp.flip(t, axis=1)


def centred_depthwise_conv(u, w, b):
    k, c = w.shape
    pad = (k - 1) // 2
    y = lax.conv_general_dilated(u, w[:, None, :], window_strides=(1,),
                                 padding=[(pad, k - 1 - pad)],
                                 dimension_numbers=('NWC', 'WIO', 'NWC'),
                                 feature_group_count=c)
    return y + b


def ssd_chunked(x, dt, a_neg, bm, cm):
    bsz, l, h, p = x.shape
    g, n = bm.shape[2], bm.shape[3]
    e = h // g
    q = SSD_CHUNK
    c = l // q
    la = (dt * a_neg).reshape(bsz, c, q, g, e)
    xs = (x * dt[..., None]).reshape(bsz, c, q, g, e, p)
    bs = bm.reshape(bsz, c, q, g, n)
    cs = cm.reshape(bsz, c, q, g, n)
    acum = jnp.cumsum(la, axis=2)
    lower = jnp.tril(jnp.ones((q, q), dtype=bool))[None, None, :, :, None, None]
    seg = acum[:, :, :, None] - acum[:, :, None, :]
    lmat = jnp.exp(jnp.where(lower, seg, -jnp.inf))
    cb = jnp.einsum('bcign,bcjgn->bcijg', cs, bs)
    y_diag = jnp.einsum('bcijge,bcjgep->bcigep', cb[..., None] * lmat, xs)
    decay_to_end = jnp.exp(acum[:, :, -1:] - acum)
    states = jnp.einsum('bcjgn,bcjge,bcjgep->bcgepn', bs, decay_to_end, xs)
    a_tot = acum[:, :, -1]
    t_inc = jnp.cumsum(a_tot, axis=1)
    t_exc = t_inc - a_tot
    before = jnp.tril(jnp.ones((c, c), dtype=bool), k=-1)[None, :, :, None, None]
    chunk_decay = jnp.exp(jnp.where(before, t_exc[:, :, None] - t_inc[:, None, :], -jnp.inf))
    s_in = jnp.einsum('bzyge,bygepn->bzgepn', chunk_decay, states)
    y_off = jnp.einsum('bcign,bcgepn,bcige->bcigep', cs, s_in, jnp.exp(acum))
    return (y_diag + y_off).reshape(bsz, l, h, p)


def ssd_mixer(z, xbc, dt_raw, conv_w, conv_b, dt_bias, a_log, d_skip, norm_w):
    bsz, l, _ = xbc.shape
    xbc = jax.nn.silu(centred_depthwise_conv(xbc, conv_w, conv_b))
    xs, bm, cm = jnp.split(xbc, [SSD_INNER, SSD_INNER + SSD_GROUPS * SSD_STATE], axis=-1)
    xs = xs.reshape(bsz, l, SSD_HEADS, SSD_HEAD_DIM)
    bm = bm.reshape(bsz, l, SSD_GROUPS, SSD_STATE)
    cm = cm.reshape(bsz, l, SSD_GROUPS, SSD_STATE)
    dt = jax.nn.softplus(dt_raw + dt_bias)
    a_neg = -jnp.exp(a_log)
    y_f = ssd_chunked(xs, dt[:, :, 0], a_neg[0], bm, cm)
    y_b = flip_seq(ssd_chunked(flip_seq(xs), flip_seq(dt[:, :, 1]), a_neg[1], flip_seq(bm), flip_seq(cm)))
    y = (y_f + y_b + xs * d_skip[:, None]).reshape(bsz, l, SSD_INNER)
    return rms_norm(y * jax.nn.silu(z), norm_w)


def axial_rope_tables(seq):
    rows = seq // GRID_W
    row = jnp.repeat(jnp.arange(rows), GRID_W).astype(jnp.float32)
    col = jnp.tile(jnp.arange(GRID_W), rows).astype(jnp.float32)
    axis_dims = ATT_HEAD_DIM // 2
    freqs = ROPE_THETA ** (-jnp.arange(0, axis_dims, 2, dtype=jnp.float32) / axis_dims)
    ang = jnp.concatenate([row[:, None] * freqs, col[:, None] * freqs], axis=-1)
    return jnp.cos(ang), jnp.sin(ang)


def apply_rope(x, cos, sin):
    x2 = x.reshape(x.shape[:-1] + (x.shape[-1] // 2, 2))
    xe, xo = x2[..., 0], x2[..., 1]
    c = cos[None, :, None, :]
    s = sin[None, :, None, :]
    out = jnp.stack([xe * c - xo * s, xe * s + xo * c], axis=-1)
    return out.reshape(x.shape)


def blocked_gqa(q, k, v):
    bsz, l, hq, d = q.shape
    hkv = k.shape[2]
    rep = hq // hkv
    nb = l // ATT_BLOCK
    scale = d ** -0.5
    qb = q.reshape(bsz, nb, ATT_BLOCK, hkv, rep, d).transpose(1, 0, 2, 3, 4, 5)

    def one_block(qblk):
        s = jnp.einsum('bqgrd,bkgd->bgrqk', qblk, k) * scale
        p = jax.nn.softmax(s, axis=-1)
        return jnp.einsum('bgrqk,bkgd->bqgrd', p, v)

    o = lax.map(one_block, qb)
    return o.transpose(1, 0, 2, 3, 4, 5).reshape(bsz, l, hq * d)


def multiscale_pool(u, pool_w, pool_scale):
    bsz, l, _ = u.shape
    csum = jnp.concatenate([jnp.zeros((bsz, 1, POOL_INNER), jnp.float32), jnp.cumsum(u, axis=1)], axis=1)
    t = jnp.arange(l)
    outs = []
    for gi, w in enumerate(POOL_WINDOWS):
        lo = jnp.clip(t - w // 2, 0, l)
        hi = jnp.clip(t - w // 2 + w, 0, l)
        cg = csum[:, :, gi * POOL_GROUP:(gi + 1) * POOL_GROUP]
        mean = (cg[:, hi] - cg[:, lo]) / (hi - lo).astype(jnp.float32)[None, :, None]
        outs.append(mean - u[:, :, gi * POOL_GROUP:(gi + 1) * POOL_GROUP])
    pooled = jnp.stack(outs, axis=2)
    mixed = jnp.einsum('blgc,gcd->blgd', pooled, pool_w).reshape(bsz, l, POOL_INNER)
    return mixed * pool_scale


def hgrn2_scan(q, logf, k, v):
    bsz, l, h, dk = q.shape
    dv = v.shape[-1]
    cq = HGRN_CHUNK
    nc = l // cq

    def to_chunks(t):
        return t.reshape(bsz, nc, cq, h, t.shape[-1]).transpose(1, 0, 3, 2, 4)

    lower = jnp.tril(jnp.ones((cq, cq), dtype=bool))[:, :, None]

    def step(state, inp):
        qc, lfc, kc, vc = inp
        bcum = jnp.cumsum(lfc, axis=2)
        diff = bcum[:, :, :, None, :] - bcum[:, :, None, :, :]
        decay = jnp.exp(jnp.where(lower, diff, -jnp.inf))
        scores = jnp.einsum('bhic,bhjc,bhijc->bhij', qc, kc, decay)
        o = jnp.einsum('bhij,bhjv->bhiv', scores, vc) + jnp.einsum('bhic,bhcv->bhiv', qc * jnp.exp(bcum), state)
        last = bcum[:, :, -1]
        k_to_end = kc * jnp.exp(last[:, :, None, :] - bcum)
        state = jnp.exp(last)[..., None] * state + jnp.einsum('bhjc,bhjv->bhcv', k_to_end, vc)
        return state, o

    s0 = jnp.zeros((bsz, h, dk, dv), jnp.float32)
    _, o = lax.scan(step, s0, (to_chunks(q), to_chunks(logf), to_chunks(k), to_chunks(v)))
    return o.transpose(1, 0, 3, 2, 4).reshape(bsz, l, h, dv)


def hgrn2_mixer(q, f_fwd_raw, f_bwd_raw, i_in, g, lb, norm_w):
    bsz, l, _ = q.shape
    heads = lambda t: t.reshape(bsz, l, HGRN_HEADS, HGRN_HEAD_DIM)
    lbh = lb.reshape(HGRN_HEADS, HGRN_HEAD_DIM)

    def gate(raw):
        raw = heads(raw)
        logf = jnp.logaddexp(jnp.log(lbh), jnp.log1p(-lbh) + jax.nn.log_sigmoid(raw))
        return logf, (1.0 - lbh) * jax.nn.sigmoid(-raw)

    qh, vh = heads(q), heads(i_in)
    lf_f, k_f = gate(f_fwd_raw)
    lf_b, k_b = gate(f_bwd_raw)
    o_f = hgrn2_scan(qh, lf_f, k_f, vh)
    o_b = flip_seq(hgrn2_scan(flip_seq(qh), flip_seq(lf_b), flip_seq(k_b), flip_seq(vh)))
    o = rms_norm(o_f + o_b, norm_w.reshape(HGRN_HEADS, HGRN_HEAD_DIM)).reshape(bsz, l, HGRN_INNER)
    return o * jax.nn.sigmoid(g)


def expert_choice_moe(x, router_w, w1, w3, w2):
    bsz, l, d = x.shape
    cap = CAPACITY_FACTOR * l // N_EXPERTS
    logits = jnp.einsum('bld,de->ble', x, router_w, precision=lax.Precision.HIGHEST)
    aff = jax.nn.softmax(logits, axis=-1)
    gate, idx = lax.top_k(aff.transpose(0, 2, 1), cap)
    xs = jax.vmap(lambda xb, ib: xb[ib])(x, idx)
    xs = xs.transpose(1, 0, 2, 3).reshape(N_EXPERTS, bsz * cap, d).astype(jnp.bfloat16)
    y = expert_ffn(xs, w1, w3, w2)
    y = y.reshape(N_EXPERTS, bsz, cap, d).transpose(1, 0, 2, 3) * gate[..., None]
    return jax.vmap(lambda yb, ib: jnp.zeros((l, d), yb.dtype).at[ib.reshape(-1)].add(yb.reshape(-1, d)))(y, idx)


def kernel(x, w_in_ab, ssm_conv_w, ssm_conv_b, ssm_dt_bias, ssm_a_log, ssm_d, ssm_norm_w, attn_q_norm, attn_k_norm, w_out_ab, w_in_cd, pool_w, pool_scale, hgrn_lb_logits, hgrn_norm_w, w_out_cd, router_w, moe_w1, moe_w3, moe_w2, ln1_g, ln1_b, ln2_g, ln2_b):
    bsz, seq, d = x.shape
    m = bsz * seq
    cos_t, sin_t = rope_lane_tables(seq)
    gq = head_mean_matrix()
    lb_all = jnp.cumsum(jax.nn.softmax(hgrn_lb_logits, axis=0), axis=0)
    lb_all = lb_all - lb_all[0]
    for layer in range(DEPTH):
        j = layer // 2
        x2 = x.reshape(m, d)
        if layer % 2 == 0:
            w = w_in_ab[j]
            w_main = jnp.concatenate([w[:, :1536], w[:, 1552:]], axis=1).astype(jnp.bfloat16)
            proj = matmul(x2, w_main, tn=768)
            dt_raw = jnp.dot(x2, w[:, 1536:1552], precision=lax.Precision.HIGHEST)
            z = proj[:, :512].reshape(bsz, seq, -1)
            xbc = proj[:, 512:1536].reshape(bsz, seq, -1)
            y_a = ssd_mixer(z, xbc, dt_raw.reshape(bsz, seq, 2, SSD_HEADS), ssm_conv_w[j], ssm_conv_b[j],
                            ssm_dt_bias[j], ssm_a_log[j], ssm_d[j], ssm_norm_w[j]).reshape(m, -1)
            y_b = gqa(proj, cos_t, sin_t, gq, attn_q_norm[j], attn_k_norm[j], bsz, seq)
            y = jnp.concatenate([y_a, y_b], axis=-1)
            mix = matmul(y, w_out_ab[j].astype(jnp.bfloat16))
        else:
            proj = matmul(x2, w_in_cd[j].astype(jnp.bfloat16), tn=768)
            u_pool = proj[:, :POOL_INNER].reshape(bsz, seq, -1)
            y_c = multiscale_pool(u_pool, pool_w[j], pool_scale[j]).reshape(m, -1)
            y_d = hgrn2(proj, lb_all[layer], hgrn_norm_w[j], bsz, seq)
            y = jnp.concatenate([y_c, y_d], axis=-1)
            mix = matmul(y, w_out_cd[j].astype(jnp.bfloat16))
        x = layer_norm(DEEPNORM_ALPHA * x + mix.reshape(bsz, seq, d), ln1_g[layer], ln1_b[layer])
        ffn = expert_choice_moe(x, router_w[layer], moe_w1[layer], moe_w3[layer], moe_w2[layer])
        x = layer_norm(DEEPNORM_ALPHA * x + ffn, ln2_g[layer], ln2_b[layer])
    return x
```

```python
import functools
import math

import jax
import jax.numpy as jnp
from jax import lax
from jax.experimental import pallas as pl
from jax.experimental.pallas import tpu as pltpu

D_MODEL = 1024
DEPTH = 4
GRID_W = 64
NORM_EPS = 1e-6
LN_EPS = 1e-5

SSD_HEADS = 8
SSD_HEAD_DIM = 64
SSD_INNER = 512
SSD_GROUPS = 2
SSD_STATE = 128
SSD_CONV_K = 5
SSD_CHUNK = 128
SSD_XBC = 1024

ATT_Q_HEADS = 8
ATT_KV_HEADS = 2
ATT_HEAD_DIM = 64
ATT_INNER = 512
ATT_BLOCK = 128
ROPE_THETA = 10000.0

POOL_WINDOWS = (2, 4, 8, 16)
POOL_GROUP = 128
POOL_INNER = 512

HGRN_HEADS = 4
HGRN_HEAD_DIM = 128
HGRN_INNER = 512
HGRN_CHUNK = 64

N_EXPERTS = 16
EXPERT_FF = 2048
CAPACITY_FACTOR = 2

DEEPNORM_ALPHA = (2 * DEPTH) ** 0.25

AB_WIDTHS = (SSD_INNER, SSD_XBC, 2 * SSD_HEADS, ATT_INNER, ATT_KV_HEADS * ATT_HEAD_DIM, ATT_KV_HEADS * ATT_HEAD_DIM)
CD_WIDTHS = (POOL_INNER, HGRN_INNER, HGRN_INNER, HGRN_INNER, HGRN_INNER, HGRN_INNER)

VMEM_LIMIT = 56 * 1024 * 1024


def _mm_kernel(x_ref, w_ref, o_ref):
    o_ref[...] = jnp.dot(x_ref[...].astype(jnp.bfloat16), w_ref[...],
                         preferred_element_type=jnp.float32)


MM_TM = 2048
MM_TN = 768


def matmul(x, w_bf16, *, tm=MM_TM, tn=MM_TN):
    m, k = x.shape
    n = w_bf16.shape[1]
    return pl.pallas_call(
        _mm_kernel,
        grid=(m // tm, n // tn),
        in_specs=[pl.BlockSpec((tm, k), lambda i, j: (i, 0)),
                  pl.BlockSpec((k, tn), lambda i, j: (0, j))],
        out_specs=pl.BlockSpec((tm, tn), lambda i, j: (i, j)),
        out_shape=jax.ShapeDtypeStruct((m, n), jnp.float32),
        compiler_params=pltpu.CompilerParams(
            dimension_semantics=("parallel", "parallel"), vmem_limit_bytes=VMEM_LIMIT),
        name="matmul",
    )(x, w_bf16)


FFN_TF = 512


FFN_ROWS = 512


def _ffn_kernel(xs_ref, w1_ref, w3_ref, w2_ref, o_ref, acc_ref):
    f = pl.program_id(1)

    @pl.when(f == 0)
    def _():
        acc_ref[...] = jnp.zeros_like(acc_ref)

    w1 = w1_ref[...].astype(jnp.bfloat16)
    w3 = w3_ref[...].astype(jnp.bfloat16)
    w2 = w2_ref[...].astype(jnp.bfloat16)
    rc = min(FFN_ROWS, xs_ref.shape[0])
    for c in range(xs_ref.shape[0] // rc):
        rows = slice(c * rc, (c + 1) * rc)
        xc = xs_ref[rows, :]
        h1 = jnp.dot(xc, w1, preferred_element_type=jnp.float32)
        h3 = jnp.dot(xc, w3, preferred_element_type=jnp.float32)
        hdn = (h1 * jax.nn.sigmoid(h1) * h3).astype(jnp.bfloat16)
        acc_ref[rows, :] += jnp.dot(hdn, w2, preferred_element_type=jnp.float32)

    @pl.when(f == pl.num_programs(1) - 1)
    def _():
        o_ref[...] = acc_ref[...].astype(o_ref.dtype)


def expert_ffn(xs, w1, w3, w2, layer):
    e, r, d = xs.shape
    ff = w1.shape[3]
    tf = FFN_TF
    return pl.pallas_call(
        _ffn_kernel,
        grid=(e, ff // tf),
        in_specs=[pl.BlockSpec((None, r, d), lambda i, j: (i, 0, 0)),
                  pl.BlockSpec((None, None, d, tf), lambda i, j: (layer, i, 0, j)),
                  pl.BlockSpec((None, None, d, tf), lambda i, j: (layer, i, 0, j)),
                  pl.BlockSpec((None, None, tf, d), lambda i, j: (layer, i, j, 0))],
        out_specs=pl.BlockSpec((None, r, d), lambda i, j: (i, 0, 0)),
        out_shape=jax.ShapeDtypeStruct((e, r, d), jnp.bfloat16),
        scratch_shapes=[pltpu.VMEM((r, d), jnp.float32)],
        compiler_params=pltpu.CompilerParams(
            dimension_semantics=("parallel", "arbitrary"), vmem_limit_bytes=VMEM_LIMIT),
        name="expert_ffn",
    )(xs, w1, w3, w2)


HG_T = 64
HG_UNROLL = 2


def _hgrn_gate(raw, log_lb, log1m_lb, one_m_lb):
    ls = jnp.minimum(raw, 0.0) - jnp.log1p(jnp.exp(-jnp.abs(raw)))
    bb = log1m_lb + ls
    mx = jnp.maximum(log_lb, bb)
    logf = mx + jnp.log1p(jnp.exp(-jnp.abs(log_lb - bb)))
    return logf, one_m_lb * jax.nn.sigmoid(-raw)


HG_S = 8


def _hgrn_block(q, k, lf, v, st_ref, backward):
    t, s8 = HG_T, HG_S
    hd = HGRN_HEAD_DIM
    bf = jnp.bfloat16
    row8 = lax.broadcasted_iota(jnp.int32, (s8, hd), 0)
    r8 = lax.broadcasted_iota(jnp.int32, (s8, t), 0)
    lt = lax.broadcasted_iota(jnp.int32, (s8, t), 1)
    sdiag, cloc = [], []
    for i in range(t // s8):
        rows = slice(i * s8, (i + 1) * s8)
        qh, kh_, lfh = q[rows], k[rows], lf[rows]
        w = jnp.zeros_like(qh)
        sc = jnp.zeros((s8, t), jnp.float32)
        for d in range(s8):
            sh = (s8 - d) % s8 if backward else d
            kd = kh_ if d == 0 else pltpu.roll(kh_, sh, axis=0)
            lfd = lfh if d == 0 else pltpu.roll(lfh, sh, axis=0)
            valid = (row8 + d <= s8 - 1) if backward else (row8 >= d)
            p = jnp.where(valid, qh * kd * jnp.exp(w), 0.0)
            col = (r8 + d if backward else r8 - d) + i * s8
            sc = jnp.where(lt == col, jnp.sum(p, axis=-1, keepdims=True), sc)
            w = w + lfd
        sdiag.append(sc)
        c = lfh
        s = 1
        while s < s8:
            if backward:
                c = c + jnp.where(row8 + s <= s8 - 1, pltpu.roll(c, s8 - s, axis=0), 0.0)
            else:
                c = c + jnp.where(row8 >= s, pltpu.roll(c, s, axis=0), 0.0)
            s *= 2
        cloc.append(c)
    scores = jnp.concatenate(sdiag, axis=0)
    cl = jnp.concatenate(cloc, axis=0)
    row = lax.broadcasted_iota(jnp.int32, (t, hd), 0)
    ti = lax.broadcasted_iota(jnp.int32, (t, t), 0)
    tj = lax.broadcasted_iota(jnp.int32, (t, t), 1)
    h = s8
    while h < t:
        is_far = ((row // h) % 2 == 0) if backward else ((row // h) % 2 == 1)
        parts = []
        for blk in range(t // (2 * h)):
            e = blk * 2 * h + (h if backward else h - 1)
            parts.append(jnp.broadcast_to(cl[e:e + 1, :], (2 * h, hd)))
        tn = jnp.concatenate(parts, axis=0) if len(parts) > 1 else parts[0]
        qt = jnp.where(is_far, q * jnp.exp(cl), 0.0).astype(bf)
        kt = jnp.where(is_far, 0.0, k * jnp.exp(jnp.where(is_far, 0.0, tn - cl))).astype(bf)
        s_h = lax.dot_general(qt, kt, NT_DIMS, preferred_element_type=jnp.float32)
        if 2 * h < t:
            s_h = jnp.where(ti // (2 * h) == tj // (2 * h), s_h, 0.0)
        scores = scores + s_h
        cl = cl + jnp.where(is_far, tn, 0.0)
        h *= 2
    edge = 0 if backward else t - 1
    tot = cl[edge:edge + 1, :]
    st = st_ref[...]
    vb = v.astype(bf)
    o = jnp.dot(scores.astype(bf), vb, preferred_element_type=jnp.float32)
    o = o + lax.dot_general((q * jnp.exp(cl)).astype(bf), st.astype(bf), NT_DIMS,
                            preferred_element_type=jnp.float32)
    kh = (k * jnp.exp(tot - cl)).astype(bf)
    ut = lax.dot_general(vb, kh, (((0,), (0,)), ((), ())), preferred_element_type=jnp.float32)
    st_ref[...] = st * jnp.exp(tot) + ut
    return o


def _hgrn_kernel(q_ref, ff_ref, fb_ref, i_ref, g_ref, lb_ref, nw_ref, o_ref,
                 lff_ref, kf_ref, lfb_ref, kb_ref, of_ref, ob_ref, stf_ref, stb_ref):
    lb = lb_ref[...]
    log_lb = jnp.log(lb)
    log1m_lb = jnp.log1p(-lb)
    one_m_lb = 1.0 - lb
    lf, kk = _hgrn_gate(ff_ref[...], log_lb, log1m_lb, one_m_lb)
    lff_ref[...] = lf
    kf_ref[...] = kk
    lf, kk = _hgrn_gate(fb_ref[...], log_lb, log1m_lb, one_m_lb)
    lfb_ref[...] = lf
    kb_ref[...] = kk
    stf_ref[...] = jnp.zeros_like(stf_ref)
    stb_ref[...] = jnp.zeros_like(stb_ref)
    nblk = q_ref.shape[0] // HG_T

    def body(n, carry):
        for u in range(HG_UNROLL):
            blk = n * HG_UNROLL + u
            rf = pl.ds(pl.multiple_of(blk * HG_T, HG_T), HG_T)
            rb = pl.ds(pl.multiple_of((nblk - 1 - blk) * HG_T, HG_T), HG_T)
            of_ref[rf, :] = _hgrn_block(q_ref[rf, :], kf_ref[rf, :], lff_ref[rf, :], i_ref[rf, :], stf_ref, False)
            ob_ref[rb, :] = _hgrn_block(q_ref[rb, :], kb_ref[rb, :], lfb_ref[rb, :], i_ref[rb, :], stb_ref, True)
        return carry

    lax.fori_loop(0, nblk // HG_UNROLL, body, 0)
    o = of_ref[...] + ob_ref[...]
    o = o * lax.rsqrt(jnp.mean(o * o, axis=-1, keepdims=True) + NORM_EPS) * nw_ref[...]
    o_ref[...] = o * jax.nn.sigmoid(g_ref[...])


def hgrn2(proj, lb, norm_w, bsz, seq):
    m = proj.shape[0]
    hd = HGRN_HEAD_DIM
    nh = HGRN_HEADS

    def col(base):
        return pl.BlockSpec((seq, hd), lambda b, h: (b, base * nh + h))

    vec = pl.BlockSpec((1, hd), lambda b, h: (0, h))
    f32 = jnp.float32
    return pl.pallas_call(
        _hgrn_kernel,
        grid=(bsz, nh),
        in_specs=[col(1), col(2), col(3), col(4), col(5), vec, vec],
        out_specs=pl.BlockSpec((seq, hd), lambda b, h: (b, h)),
        out_shape=jax.ShapeDtypeStruct((m, HGRN_INNER), f32),
        scratch_shapes=[pltpu.VMEM((seq, hd), f32) for _ in range(6)]
        + [pltpu.VMEM((hd, hd), f32) for _ in range(2)],
        compiler_params=pltpu.CompilerParams(
            dimension_semantics=("parallel", "parallel"), vmem_limit_bytes=VMEM_LIMIT),
        name="hgrn2",
    )(proj, proj, proj, proj, proj, lb.reshape(1, -1), norm_w.reshape(1, -1))


ATT_TQ = 256
LANES = 128


def _group_mean_sq(x, g_ref):
    x2 = x * x
    hi = x2.astype(jnp.bfloat16)
    lo = (x2 - hi.astype(jnp.float32)).astype(jnp.bfloat16)
    g = g_ref[...]
    return (jnp.dot(hi, g, preferred_element_type=jnp.float32)
            + jnp.dot(lo, g, preferred_element_type=jnp.float32))


def _norm_rope(x, g_ref, w, cos, sin_signed):
    n = x.shape[1]
    y = x * lax.rsqrt(_group_mean_sq(x, g_ref) + NORM_EPS) * w
    lane = lax.broadcasted_iota(jnp.int32, y.shape, 1)
    partner = jnp.where(lane % 2 == 0, pltpu.roll(y, n - 1, axis=1), pltpu.roll(y, 1, axis=1))
    return y * cos + partner * sin_signed


def _dup_halves(x):
    lane = lax.broadcasted_iota(jnp.int32, x.shape, 1)
    sw = pltpu.roll(x, ATT_HEAD_DIM, axis=1)
    lo = lane < ATT_HEAD_DIM
    return jnp.where(lo, x, sw), jnp.where(lo, sw, x)


def _gqa_kernel(q_ref, k_ref, v_ref, cq_ref, sq_ref, ck_ref, sk_ref, gq_ref, gk_ref, qw_ref, kw_ref,
                o_ref, kd_ref, vd_ref):
    @pl.when(pl.program_id(1) == 0)
    def _():
        kr = _norm_rope(k_ref[...], gk_ref, kw_ref[...], ck_ref[...], sk_ref[...])
        k0, k1 = _dup_halves(kr)
        kd_ref[0] = k0.astype(jnp.bfloat16)
        kd_ref[1] = k1.astype(jnp.bfloat16)
        v0, v1 = _dup_halves(v_ref[...])
        vd_ref[0] = v0.astype(jnp.bfloat16)
        vd_ref[1] = v1.astype(jnp.bfloat16)

    tq = q_ref.shape[0]
    reps = ATT_INNER // LANES
    cos = jnp.concatenate([cq_ref[...]] * reps, axis=1)
    sin = jnp.concatenate([sq_ref[...]] * reps, axis=1)
    qr = _norm_rope(q_ref[...], gq_ref, qw_ref[...], cos, sin) * (ATT_HEAD_DIM ** -0.5)
    lane = lax.broadcasted_iota(jnp.int32, (tq, LANES), 1)
    lo = lane < ATT_HEAD_DIM
    rep = ATT_Q_HEADS // ATT_KV_HEADS
    for pair in range(ATT_Q_HEADS // 2):
        grp = (2 * pair) // rep
        qp = qr[:, pair * LANES:(pair + 1) * LANES]
        q2 = jnp.concatenate([jnp.where(lo, qp, 0.0), jnp.where(lo, 0.0, qp)], axis=0).astype(jnp.bfloat16)
        s = lax.dot_general(q2, kd_ref[grp], (((1,), (1,)), ((), ())), preferred_element_type=jnp.float32)
        s = s - jnp.max(s, axis=-1, keepdims=True)
        p = jnp.exp(s)
        l = jnp.sum(p, axis=-1, keepdims=True)
        o2 = jnp.dot(p.astype(jnp.bfloat16), vd_ref[grp], preferred_element_type=jnp.float32) / l
        o_ref[:, pair * LANES:(pair + 1) * LANES] = jnp.where(lo, o2[:tq], o2[tq:])


def gqa(proj, cos_t, sin_t, gq, q_w, k_w, bsz, seq):
    m = proj.shape[0]
    tq = ATT_TQ
    nq = seq // tq
    kvw = ATT_KV_HEADS * ATT_HEAD_DIM
    full = lambda shape: pl.BlockSpec(shape, lambda b, i: (0, 0))
    return pl.pallas_call(
        _gqa_kernel,
        grid=(bsz, nq),
        in_specs=[pl.BlockSpec((tq, ATT_INNER), lambda b, i: (b * nq + i, 1536 // ATT_INNER)),
                  pl.BlockSpec((seq, kvw), lambda b, i: (b, 2048 // kvw)),
                  pl.BlockSpec((seq, kvw), lambda b, i: (b, 2176 // kvw)),
                  pl.BlockSpec((tq, LANES), lambda b, i: (i, 0)),
                  pl.BlockSpec((tq, LANES), lambda b, i: (i, 0)),
                  full((seq, LANES)), full((seq, LANES)),
                  full((ATT_INNER, ATT_INNER)), full((LANES, LANES)),
                  full((1, ATT_INNER)), full((1, LANES))],
        out_specs=pl.BlockSpec((tq, ATT_INNER), lambda b, i: (b * nq + i, 0)),
        out_shape=jax.ShapeDtypeStruct((m, ATT_INNER), jnp.float32),
        scratch_shapes=[pltpu.VMEM((ATT_KV_HEADS, seq, LANES), jnp.bfloat16),
                        pltpu.VMEM((ATT_KV_HEADS, seq, LANES), jnp.bfloat16)],
        compiler_params=pltpu.CompilerParams(
            dimension_semantics=("parallel", "arbitrary"), vmem_limit_bytes=VMEM_LIMIT),
        name="gqa",
    )(proj, proj, proj, cos_t, sin_t, cos_t, sin_t, gq, gq[:LANES, :LANES],
      jnp.tile(q_w, ATT_Q_HEADS).reshape(1, -1), jnp.tile(k_w, ATT_KV_HEADS).reshape(1, -1))


def rope_lane_tables(seq):
    cos, sin = axial_rope_tables(seq)
    cos2 = jnp.repeat(cos, 2, axis=1)
    sin2 = jnp.repeat(sin, 2, axis=1) * jnp.tile(jnp.array([-1.0, 1.0], jnp.float32), ATT_HEAD_DIM // 2)
    return jnp.tile(cos2, (1, 2)), jnp.tile(sin2, (1, 2))


def head_mean_matrix():
    idx = jnp.arange(ATT_INNER) // ATT_HEAD_DIM
    return jnp.where(idx[:, None] == idx[None, :], 1.0 / ATT_HEAD_DIM, 0.0).astype(jnp.bfloat16)


NT_DIMS = (((1,), (1,)), ((), ()))
LN_TM = 512


def _layer_norm(x, g, b):
    mu = jnp.mean(x, axis=-1, keepdims=True)
    xc = x - mu
    var = jnp.mean(xc * xc, axis=-1, keepdims=True)
    return xc * lax.rsqrt(var + LN_EPS) * g + b


def _bf16_split(x):
    hi = x.astype(jnp.bfloat16)
    return hi, (x - hi.astype(jnp.float32)).astype(jnp.bfloat16)


def _mix_ln_router_kernel(ya_ref, yb_ref, x_ref, nw_ref, wa_ref, wb_ref, g_ref, b_ref, rwh_ref, rwl_ref,
                          x1_ref, x1b_ref, aff_ref, *, rms_ya):
    ya = ya_ref[...]
    if rms_ya:
        ya = ya * lax.rsqrt(jnp.mean(ya * ya, axis=-1, keepdims=True) + NORM_EPS) * nw_ref[...]
    mix = (jnp.dot(ya.astype(jnp.bfloat16), wa_ref[...], preferred_element_type=jnp.float32)
           + jnp.dot(yb_ref[...].astype(jnp.bfloat16), wb_ref[...], preferred_element_type=jnp.float32))
    x1 = _layer_norm(DEEPNORM_ALPHA * x_ref[...] + mix, g_ref[...], b_ref[...])
    x1_ref[...] = x1
    x1b_ref[...] = x1.astype(jnp.bfloat16)
    hi, lo = _bf16_split(x1)
    rwh = rwh_ref[...]
    lt = (lax.dot_general(rwh, hi, NT_DIMS, preferred_element_type=jnp.float32)
          + lax.dot_general(rwl_ref[...], hi, NT_DIMS, preferred_element_type=jnp.float32)
          + lax.dot_general(rwh, lo, NT_DIMS, preferred_element_type=jnp.float32))
    e = jnp.exp(lt - jnp.max(lt, axis=0, keepdims=True))
    aff_ref[0] = e / jnp.sum(e, axis=0, keepdims=True)


def mix_ln_router(ya, yb, x, w_out, ln_g, ln_b, router_w, bsz, seq, ya_norm_w=None):
    m, d = x.shape
    tm = LN_TM
    nt = seq // tm
    ka = ya.shape[1]
    rms_ya = ya_norm_w is not None
    nw = (ya_norm_w if rms_ya else jnp.ones((ka,), jnp.float32)).reshape(1, ka)
    wa = w_out[:ka].astype(jnp.bfloat16)
    wb = w_out[ka:].astype(jnp.bfloat16)
    rwh, rwl = _bf16_split(router_w.T)
    ne = router_w.shape[1]
    row = lambda w: pl.BlockSpec((tm, w), lambda i: (i, 0))
    full = lambda a: pl.BlockSpec(a.shape, lambda i: (0, 0))
    g2, b2 = ln_g.reshape(1, d), ln_b.reshape(1, d)
    return pl.pallas_call(
        functools.partial(_mix_ln_router_kernel, rms_ya=rms_ya),
        grid=(m // tm,),
        in_specs=[row(ka), row(yb.shape[1]), row(d), full(nw), full(wa), full(wb), full(g2), full(b2),
                  full(rwh), full(rwl)],
        out_specs=[row(d), row(d), pl.BlockSpec((1, ne, tm), lambda i: (i // nt, 0, i % nt))],
        out_shape=[jax.ShapeDtypeStruct((m, d), jnp.float32),
                   jax.ShapeDtypeStruct((m, d), jnp.bfloat16),
                   jax.ShapeDtypeStruct((bsz, ne, seq), jnp.float32)],
        compiler_params=pltpu.CompilerParams(
            dimension_semantics=("parallel",), vmem_limit_bytes=VMEM_LIMIT),
        name="mix_ln_router",
    )(ya, yb, x, nw, wa, wb, g2, b2, rwh, rwl)


ROUTE_BLK = 256


def _excl_cumsum_lanes(mask, tri):
    n = mask.shape[1]
    blk = tri.shape[0]
    run = jnp.zeros((mask.shape[0], 1), jnp.float32)
    outs = []
    for k in range(n // blk):
        mb = mask[:, k * blk:(k + 1) * blk]
        outs.append(jnp.dot(mb.astype(jnp.bfloat16), tri, preferred_element_type=jnp.float32) + run)
        run = run + jnp.sum(mb, axis=-1, keepdims=True)
    return jnp.concatenate(outs, axis=1)


def _route_kernel(aff_ref, tri_ref, pos_ref, gsel_ref, *, cap):
    a = aff_ref[0]
    capf = jnp.float32(cap)
    bits = jnp.zeros((a.shape[0], 1), jnp.int32)
    for bit in range(30, -1, -1):
        cand = bits | (1 << bit)
        cnt = jnp.sum(jnp.where(a >= pltpu.bitcast(cand, jnp.float32), 1.0, 0.0), axis=-1, keepdims=True)
        bits = jnp.where(cnt >= capf, cand, bits)
    thr = pltpu.bitcast(bits, jnp.float32)
    gt = jnp.where(a > thr, 1.0, 0.0)
    eq = jnp.where(a == thr, 1.0, 0.0)
    need = capf - jnp.sum(gt, axis=-1, keepdims=True)
    tri = tri_ref[...]
    sel = gt + eq * jnp.where(_excl_cumsum_lanes(eq, tri) < need, 1.0, 0.0)
    pos = _excl_cumsum_lanes(sel, tri)
    chosen = sel > 0.5
    pos_ref[0] = jnp.where(chosen, pos, -1.0)
    gsel_ref[0] = jnp.where(chosen, a, 0.0)


def route(aff, cap):
    bsz, ne, seq = aff.shape
    blk = min(ROUTE_BLK, seq)
    idx = jnp.arange(blk)
    tri = (idx[:, None] < idx[None, :]).astype(jnp.bfloat16)
    spec = pl.BlockSpec((1, ne, seq), lambda b: (b, 0, 0))
    return pl.pallas_call(
        functools.partial(_route_kernel, cap=cap),
        grid=(bsz,),
        in_specs=[spec, pl.BlockSpec((blk, blk), lambda b: (0, 0))],
        out_specs=[spec, spec],
        out_shape=[jax.ShapeDtypeStruct((bsz, ne, seq), jnp.float32),
                   jax.ShapeDtypeStruct((bsz, ne, seq), jnp.float32)],
        compiler_params=pltpu.CompilerParams(
            dimension_semantics=("parallel",), vmem_limit_bytes=VMEM_LIMIT),
        name="route",
    )(aff, tri)


def _gather_kernel(pos_ref, x_ref, o_ref):
    pos = pos_ref[0, 0]
    cap = o_ref.shape[1]
    slot = lax.broadcasted_iota(jnp.int32, (cap, pos.shape[1]), 0).astype(jnp.float32)
    onehot = jnp.where(pos == slot, 1.0, 0.0).astype(jnp.bfloat16)
    o_ref[0] =jnp.dot(onehot, x_ref[...], preferred_element_type=jnp.float32).astype(jnp.bfloat16)


def gather_tokens(pos, x1b, cap):
    bsz, ne, seq = pos.shape
    d = x1b.shape[1]
    return pl.pallas_call(
        _gather_kernel,
        grid=(bsz, ne),
        in_specs=[pl.BlockSpec((1, 1, 1, seq), lambda b, e: (b, e, 0, 0)),
                  pl.BlockSpec((seq, d), lambda b, e: (b, 0))],
        out_specs=pl.BlockSpec((1, cap, d), lambda b, e: (e, b, 0)),
        out_shape=jax.ShapeDtypeStruct((ne, bsz * cap, d), jnp.bfloat16),
        compiler_params=pltpu.CompilerParams(
            dimension_semantics=("parallel", "parallel"), vmem_limit_bytes=VMEM_LIMIT),
        name="gather_tokens",
    )(pos.reshape(bsz, ne, 1, seq), x1b)


COMB_TR = 256


def _combine_ln_kernel(ys_ref, post_ref, gselt_ref, x1_ref, g_ref, b_ref, o_ref):
    post = post_ref[0]
    gsel = gselt_ref[0]
    ne, cap, _ = ys_ref.shape
    lane = lax.broadcasted_iota(jnp.int32, (post.shape[0], cap), 1).astype(jnp.float32)
    acc = jnp.zeros(o_ref.shape, jnp.float32)
    for e in range(ne):
        onehot = jnp.where(post[:, e:e + 1] == lane, 1.0, 0.0).astype(jnp.bfloat16)
        acc = acc + gsel[:, e:e + 1] * jnp.dot(onehot, ys_ref[e], preferred_element_type=jnp.float32)
    o_ref[...] = _layer_norm(DEEPNORM_ALPHA * x1_ref[...] + acc, g_ref[...], b_ref[...])


def combine_ln(ys, post, gselt, x1, ln_g, ln_b, cap):
    bsz, seq, ne = post.shape
    m, d = x1.shape
    tr = min(COMB_TR, seq)
    nr = seq // tr
    g2, b2 = ln_g.reshape(1, d), ln_b.reshape(1, d)
    tok = pl.BlockSpec((1, tr, ne), lambda b, r: (b, r, 0))
    vec = pl.BlockSpec((1, d), lambda b, r: (0, 0))
    rows = pl.BlockSpec((tr, d), lambda b, r: (b * nr + r, 0))
    return pl.pallas_call(
        _combine_ln_kernel,
        grid=(bsz, nr),
        in_specs=[pl.BlockSpec((ne, cap, d), lambda b, r: (0, b, 0)), tok, tok, rows, vec, vec],
        out_specs=rows,
        out_shape=jax.ShapeDtypeStruct((m, d), jnp.float32),
        compiler_params=pltpu.CompilerParams(
            dimension_semantics=("parallel", "parallel"), vmem_limit_bytes=VMEM_LIMIT),
        name="combine_ln",
    )(ys, post, gselt, x1, g2, b2)


def moe_block(ya, yb, x2d, w_out, ln1_g, ln1_b, router_w, w1, w3, w2, layer, ln2_g, ln2_b, bsz, seq,
              ya_norm_w=None):
    cap = CAPACITY_FACTOR * seq // N_EXPERTS
    x1, x1b, aff = mix_ln_router(ya, yb, x2d, w_out, ln1_g, ln1_b, router_w, bsz, seq, ya_norm_w)
    pos, gsel = route(aff, cap)
    xs = gather_tokens(pos, x1b, cap)
    ys = expert_ffn(xs, w1, w3, w2, layer)
    return combine_ln(ys, pos.transpose(0, 2, 1), gsel.transpose(0, 2, 1), x1, ln2_g, ln2_b, cap)


SSD_GH = SSD_HEADS // SSD_GROUPS
SSD_GW = SSD_GH * SSD_HEAD_DIM


def _shift_rows(x, s, row):
    n = x.shape[0]
    if s == 0:
        return x
    y = pltpu.roll(x, (-s) % n, axis=0)
    return jnp.where(row >= -s, y, 0.0) if s < 0 else jnp.where(row < n - s, y, 0.0)


def _conv_silu(x, w_ref, b_ref):
    row = lax.broadcasted_iota(jnp.int32, x.shape, 0)
    pad = (SSD_CONV_K - 1) // 2
    y = b_ref[...] + jnp.zeros_like(x)
    for k in range(SSD_CONV_K):
        y = y + w_ref[k:k + 1, :] * _shift_rows(x, k - pad, row)
    return y * jax.nn.sigmoid(y)


def _cumsum_rows(x, row, reverse):
    n = x.shape[0]
    s = 1
    while s < n:
        if reverse:
            x = x + jnp.where(row + s < n, pltpu.roll(x, n - s, axis=0), 0.0)
        else:
            x = x + jnp.where(row >= s, pltpu.roll(x, s, axis=0), 0.0)
        s *= 2
    return x


def _expand_heads(v, cols, lane_lo):
    parts = []
    for p in range(len(cols) // 2):
        a = v[:, cols[2 * p]:cols[2 * p] + 1]
        b = v[:, cols[2 * p + 1]:cols[2 * p + 1] + 1]
        parts.append(jnp.where(lane_lo, a, b))
    return jnp.concatenate(parts, axis=1)


def _ssd_chunk(ux, ub, uc, dtv, la, st_ref, reverse):
    q = SSD_CHUNK
    bf = jnp.bfloat16
    cols = [SSD_GH + h if reverse else h for h in range(SSD_GH)]
    row = lax.broadcasted_iota(jnp.int32, (q, LANES), 0)
    lane = lax.broadcasted_iota(jnp.int32, (q, LANES), 1)
    lane_lo = lane < SSD_HEAD_DIM
    tri = (row <= lane) if reverse else (row >= lane)
    cum = _cumsum_rows(la, row, reverse)
    cum_t = cum.T
    tot = cum[0:1, :] if reverse else cum[q - 1:q, :]
    ub16 = ub.astype(bf)
    uc16 = uc.astype(bf)
    cb = lax.dot_general(uc16, ub16, NT_DIMS, preferred_element_type=jnp.float32)
    xdt = ux * _expand_heads(dtv, cols, lane_lo)
    ys = []
    for p in range(SSD_GH // 2):
        xp = xdt[:, p * LANES:(p + 1) * LANES]
        acc = None
        for half in range(2):
            c = cols[2 * p + half]
            seg = cum[:, c:c + 1] - cum_t[c:c + 1, :]
            lm = jnp.where(tri, jnp.exp(seg), 0.0)
            xh = jnp.where(lane_lo, xp, 0.0) if half == 0 else jnp.where(lane_lo, 0.0, xp)
            t = jnp.dot((cb * lm).astype(bf), xh.astype(bf), preferred_element_type=jnp.float32)
            acc = t if acc is None else acc + t
        ys.append(acc)
    y = jnp.concatenate(ys, axis=1)
    st = st_ref[...]
    y = y + _expand_heads(jnp.exp(cum), cols, lane_lo) * jnp.dot(uc16, st.astype(bf), preferred_element_type=jnp.float32)
    xe = (xdt * _expand_heads(jnp.exp(tot - cum), cols, lane_lo)).astype(bf)
    new = lax.dot_general(ub16, xe, (((0,), (0,)), ((), ())), preferred_element_type=jnp.float32)
    st_ref[...] = st * _expand_heads(jnp.exp(tot), cols, lane_lo[0:1]) + new
    return y


def _ssd_kernel(z_ref, x_ref, b_ref, c_ref, dt_ref, cwx_ref, cwb_ref, cwc_ref, cbx_ref, cbb_ref, cbc_ref,
                dtb_ref, aneg_ref, dsk_ref, o_ref, ux_ref, ub_ref, uc_ref, dtv_ref, la_ref, yf_ref, yb_ref,
                stf_ref, stb_ref):
    ux_ref[...] = _conv_silu(x_ref[...], cwx_ref, cbx_ref)
    ub_ref[...] = _conv_silu(b_ref[...], cwb_ref, cbb_ref)
    uc_ref[...] = _conv_silu(c_ref[...], cwc_ref, cbc_ref)
    t = dt_ref[...] + dtb_ref[...]
    dtv = jnp.maximum(t, 0.0) + jnp.log1p(jnp.exp(-jnp.abs(t)))
    dtv_ref[...] = dtv
    la_ref[...] = dtv * aneg_ref[...]
    stf_ref[...] = jnp.zeros_like(stf_ref)
    stb_ref[...] = jnp.zeros_like(stb_ref)
    q = SSD_CHUNK
    nchunk = x_ref.shape[0] // q

    def body(c, carry):
        rf = pl.ds(pl.multiple_of(c * q, q), q)
        rb = pl.ds(pl.multiple_of((nchunk - 1 - c) * q, q), q)
        yf_ref[rf, :] = _ssd_chunk(ux_ref[rf, :], ub_ref[rf, :], uc_ref[rf, :], dtv_ref[rf, :], la_ref[rf, :], stf_ref, False)
        yb_ref[rb, :] = _ssd_chunk(ux_ref[rb, :], ub_ref[rb, :], uc_ref[rb, :], dtv_ref[rb, :], la_ref[rb, :], stb_ref, True)
        return carry

    lax.fori_loop(0, nchunk, body, 0)
    z = z_ref[...]
    y = yf_ref[...] + yb_ref[...] + ux_ref[...] * dsk_ref[...]
    o_ref[...] = y * (z * jax.nn.sigmoid(z))


def ssd(proj, dt, conv_w, conv_b, dt_bias_g, a_neg_g, d_skip_g, bsz, seq):
    m = proj.shape[0]
    gw, n = SSD_GW, SSD_STATE
    f32 = jnp.float32
    kk = SSD_CONV_K
    cb2 = conv_b.reshape(1, -1)
    in_specs = [
        pl.BlockSpec((seq, gw), lambda b, g: (b, g)),
        pl.BlockSpec((seq, gw), lambda b, g: (b, SSD_INNER // gw + g)),
        pl.BlockSpec((seq, n), lambda b, g: (b, 2 * SSD_INNER // n + g)),
        pl.BlockSpec((seq, n), lambda b, g: (b, 2 * SSD_INNER // n + SSD_GROUPS + g)),
        pl.BlockSpec((seq, LANES), lambda b, g: (b, g)),
        pl.BlockSpec((kk, gw), lambda b, g: (0, g)),
        pl.BlockSpec((kk, n), lambda b, g: (0, SSD_INNER // n + g)),
        pl.BlockSpec((kk, n), lambda b, g: (0, SSD_INNER // n + SSD_GROUPS + g)),
        pl.BlockSpec((1, gw), lambda b, g: (0, g)),
        pl.BlockSpec((1, n), lambda b, g: (0, SSD_INNER // n + g)),
        pl.BlockSpec((1, n), lambda b, g: (0, SSD_INNER // n + SSD_GROUPS + g)),
        pl.BlockSpec((1, LANES), lambda b, g: (0, g)),
        pl.BlockSpec((1, LANES), lambda b, g: (0, g)),
        pl.BlockSpec((1, gw), lambda b, g: (0, g)),
    ]
    return pl.pallas_call(
        _ssd_kernel,
        grid=(bsz, SSD_GROUPS),
        in_specs=in_specs,
        out_specs=pl.BlockSpec((seq, gw), lambda b, g: (b, g)),
        out_shape=jax.ShapeDtypeStruct((m, SSD_INNER), f32),
        scratch_shapes=[pltpu.VMEM((seq, gw), f32), pltpu.VMEM((seq, n), f32), pltpu.VMEM((seq, n), f32),
                        pltpu.VMEM((seq, LANES), f32), pltpu.VMEM((seq, LANES), f32),
                        pltpu.VMEM((seq, gw), f32), pltpu.VMEM((seq, gw), f32),
                        pltpu.VMEM((n, gw), f32), pltpu.VMEM((n, gw), f32)],
        compiler_params=pltpu.CompilerParams(
            dimension_semantics=("parallel", "parallel"), vmem_limit_bytes=VMEM_LIMIT),
        name="ssd",
    )(proj, proj, proj, proj, dt, conv_w, conv_w, conv_w, cb2, cb2, cb2, dt_bias_g, a_neg_g, d_skip_g)


def _group_lanes(v):
    out = jnp.zeros((SSD_GROUPS, LANES), v.dtype)
    for g in range(SSD_GROUPS):
        out = out.at[g, :SSD_GH].set(v[0, g * SSD_GH:(g + 1) * SSD_GH])
        out = out.at[g, SSD_GH:2 * SSD_GH].set(v[1, g * SSD_GH:(g + 1) * SSD_GH])
    return out.reshape(1, -1)


def _dt_weight(w_dt):
    d = w_dt.shape[0]
    out = jnp.zeros((d, SSD_GROUPS, LANES), w_dt.dtype)
    for g in range(SSD_GROUPS):
        out = out.at[:, g, :SSD_GH].set(w_dt[:, g * SSD_GH:(g + 1) * SSD_GH])
        out = out.at[:, g, SSD_GH:2 * SSD_GH].set(w_dt[:, SSD_HEADS + g * SSD_GH:SSD_HEADS + (g + 1) * SSD_GH])
    return out.reshape(d, -1)


def _dt_proj_kernel(x_ref, wh_ref, wl_ref, o_ref):
    hi, lo = _bf16_split(x_ref[...])
    wh = wh_ref[...]
    o_ref[...] = (jnp.dot(hi, wh, preferred_element_type=jnp.float32)
                  + jnp.dot(hi, wl_ref[...], preferred_element_type=jnp.float32)
                  + jnp.dot(lo, wh, preferred_element_type=jnp.float32))


def dt_proj(x, w_dt_g):
    m, k = x.shape
    n = w_dt_g.shape[1]
    wh, wl = _bf16_split(w_dt_g)
    tm = LN_TM
    return pl.pallas_call(
        _dt_proj_kernel,
        grid=(m // tm,),
        in_specs=[pl.BlockSpec((tm, k), lambda i: (i, 0)),
                  pl.BlockSpec((k, n), lambda i: (0, 0)), pl.BlockSpec((k, n), lambda i: (0, 0))],
        out_specs=pl.BlockSpec((tm, n), lambda i: (i, 0)),
        out_shape=jax.ShapeDtypeStruct((m, n), jnp.float32),
        compiler_params=pltpu.CompilerParams(
            dimension_semantics=("parallel",), vmem_limit_bytes=VMEM_LIMIT),
        name="dt_proj",
    )(x, wh, wl)


def _pool_kernel(u_ref, w_ref, sc_ref, o_ref):
    n = u_ref.shape[0]
    row = lax.broadcasted_iota(jnp.int32, (n, POOL_GROUP), 0)
    for gi, win in enumerate(POOL_WINDOWS):
        u = u_ref[:, gi * POOL_GROUP:(gi + 1) * POOL_GROUP]
        half = win // 2
        fwd, bwd = u, u
        s = 1
        while s < half:
            fwd = fwd + _shift_rows(fwd, s, row)
            bwd = bwd + _shift_rows(bwd, -s, row)
            s *= 2
        total = fwd + _shift_rows(bwd, -1, row)
        cnt = (jnp.minimum(row + half, n) - jnp.maximum(row - half, 0)).astype(jnp.float32)
        pooled = total / cnt - u
        mixed = jnp.dot(pooled.astype(jnp.bfloat16), w_ref[gi], preferred_element_type=jnp.float32)
        o_ref[:, gi * POOL_GROUP:(gi + 1) * POOL_GROUP] = mixed * sc_ref[:, gi * POOL_GROUP:(gi + 1) * POOL_GROUP]


def multiscale_pool_kernel(proj, pool_w, pool_scale, bsz, seq):
    m = proj.shape[0]
    return pl.pallas_call(
        _pool_kernel,
        grid=(bsz,),
        in_specs=[pl.BlockSpec((seq, POOL_INNER), lambda b: (b, 0)),
                  pl.BlockSpec(pool_w.shape, lambda b: (0, 0, 0)),
                  pl.BlockSpec((1, POOL_INNER), lambda b: (0, 0))],
        out_specs=pl.BlockSpec((seq, POOL_INNER), lambda b: (b, 0)),
        out_shape=jax.ShapeDtypeStruct((m, POOL_INNER), jnp.float32),
        compiler_params=pltpu.CompilerParams(
            dimension_semantics=("parallel",), vmem_limit_bytes=VMEM_LIMIT),
        name="multiscale_pool",
    )(proj, pool_w.astype(jnp.bfloat16), pool_scale.reshape(1, -1))


def axial_rope_tables(seq):
    rows = seq // GRID_W
    row = jnp.repeat(jnp.arange(rows), GRID_W).astype(jnp.float32)
    col = jnp.tile(jnp.arange(GRID_W), rows).astype(jnp.float32)
    axis_dims = ATT_HEAD_DIM // 2
    freqs = ROPE_THETA ** (-jnp.arange(0, axis_dims, 2, dtype=jnp.float32) / axis_dims)
    ang = jnp.concatenate([row[:, None] * freqs, col[:, None] * freqs], axis=-1)
    return jnp.cos(ang), jnp.sin(ang)


def kernel(x, w_in_ab, ssm_conv_w, ssm_conv_b, ssm_dt_bias, ssm_a_log, ssm_d, ssm_norm_w, attn_q_norm, attn_k_norm, w_out_ab, w_in_cd, pool_w, pool_scale, hgrn_lb_logits, hgrn_norm_w, w_out_cd, router_w, moe_w1, moe_w3, moe_w2, ln1_g, ln1_b, ln2_g, ln2_b):
    bsz, seq, d = x.shape
    m = bsz * seq
    cos_t, sin_t = rope_lane_tables(seq)
    gq = head_mean_matrix()
    lb_all = jnp.cumsum(jax.nn.softmax(hgrn_lb_logits, axis=0), axis=0)
    lb_all = lb_all - lb_all[0]
    x2 = x.reshape(m, d)
    for layer in range(DEPTH):
        j = layer // 2
        if layer % 2 == 0:
            w = w_in_ab[j]
            w_main = jnp.concatenate([w[:, :1536], w[:, 1552:]], axis=1).astype(jnp.bfloat16)
            proj = matmul(x2, w_main)
            dt = dt_proj(x2, _dt_weight(w[:, 1536:1552]))
            ya = ssd(proj, dt, ssm_conv_w[j], ssm_conv_b[j], _group_lanes(ssm_dt_bias[j]),
                     _group_lanes(-jnp.exp(ssm_a_log[j])), jnp.repeat(ssm_d[j], SSD_HEAD_DIM).reshape(1, -1),
                     bsz, seq)
            yb = gqa(proj, cos_t, sin_t, gq, attn_q_norm[j], attn_k_norm[j], bsz, seq)
            w_out, ya_norm_w = w_out_ab[j], ssm_norm_w[j]
        else:
            proj = matmul(x2, w_in_cd[j].astype(jnp.bfloat16))
            ya = multiscale_pool_kernel(proj, pool_w[j], pool_scale[j], bsz, seq)
            yb = hgrn2(proj, lb_all[layer], hgrn_norm_w[j], bsz, seq)
            w_out, ya_norm_w = w_out_cd[j], None
        x2 = moe_block(ya, yb, x2, w_out, ln1_g[layer], ln1_b[layer], router_w[layer],
                       moe_w1, moe_w3, moe_w2, layer, ln2_g[layer], ln2_b[layer], bsz, seq, ya_norm_w)
    return x2.reshape(bsz, seq, d)
```

```python
import functools
import math

import jax
import jax.numpy as jnp
from jax import lax
from jax.experimental import pallas as pl
from jax.experimental.pallas import tpu as pltpu

D_MODEL = 1024
DEPTH = 4
GRID_W = 64
NORM_EPS = 1e-6
LN_EPS = 1e-5

SSD_HEADS = 8
SSD_HEAD_DIM = 64
SSD_INNER = 512
SSD_GROUPS = 2
SSD_STATE = 128
SSD_CONV_K = 5
SSD_CHUNK = 128
SSD_XBC = 1024

ATT_Q_HEADS = 8
ATT_KV_HEADS = 2
ATT_HEAD_DIM = 64
ATT_INNER = 512
ATT_BLOCK = 128
ROPE_THETA = 10000.0

POOL_WINDOWS = (2, 4, 8, 16)
POOL_GROUP = 128
POOL_INNER = 512

HGRN_HEADS = 4
HGRN_HEAD_DIM = 128
HGRN_INNER = 512
HGRN_CHUNK = 64

N_EXPERTS = 16
EXPERT_FF = 2048
CAPACITY_FACTOR = 2

DEEPNORM_ALPHA = (2 * DEPTH) ** 0.25

AB_WIDTHS = (SSD_INNER, SSD_XBC, 2 * SSD_HEADS, ATT_INNER, ATT_KV_HEADS * ATT_HEAD_DIM, ATT_KV_HEADS * ATT_HEAD_DIM)
CD_WIDTHS = (POOL_INNER, HGRN_INNER, HGRN_INNER, HGRN_INNER, HGRN_INNER, HGRN_INNER)

VMEM_LIMIT = 56 * 1024 * 1024


def _mm_kernel(x_ref, w_ref, o_ref):
    o_ref[...] = jnp.dot(x_ref[...].astype(jnp.bfloat16), w_ref[...],
                         preferred_element_type=jnp.float32)


MM_TM = 2048
MM_TN = 768


def matmul(x, w_bf16, *, tm=MM_TM, tn=MM_TN):
    m, k = x.shape
    n = w_bf16.shape[1]
    return pl.pallas_call(
        _mm_kernel,
        grid=(m // tm, n // tn),
        in_specs=[pl.BlockSpec((tm, k), lambda i, j: (i, 0)),
                  pl.BlockSpec((k, tn), lambda i, j: (0, j))],
        out_specs=pl.BlockSpec((tm, tn), lambda i, j: (i, j)),
        out_shape=jax.ShapeDtypeStruct((m, n), jnp.float32),
        compiler_params=pltpu.CompilerParams(
            dimension_semantics=("parallel", "parallel"), vmem_limit_bytes=VMEM_LIMIT),
        name="matmul",
    )(x, w_bf16)


FFN_TF = 512


FFN_ROWS = 512


def _ffn_kernel(xs_ref, gs_ref, w1_ref, w3_ref, w2_ref, o_ref, acc_ref):
    f = pl.program_id(1)

    @pl.when(f == 0)
    def _():
        acc_ref[...] = jnp.zeros_like(acc_ref)

    w1 = w1_ref[...].astype(jnp.bfloat16)
    w3 = w3_ref[...].astype(jnp.bfloat16)
    w2 = w2_ref[...].astype(jnp.bfloat16)
    rc = min(FFN_ROWS, xs_ref.shape[0])
    for c in range(xs_ref.shape[0] // rc):
        rows = slice(c * rc, (c + 1) * rc)
        xc = xs_ref[rows, :]
        h1 = jnp.dot(xc, w1, preferred_element_type=jnp.float32)
        h3 = jnp.dot(xc, w3, preferred_element_type=jnp.float32)
        hdn = (h1 * jax.nn.sigmoid(h1) * h3).astype(jnp.bfloat16)
        acc_ref[rows, :] += jnp.dot(hdn, w2, preferred_element_type=jnp.float32)

    @pl.when(f == pl.num_programs(1) - 1)
    def _():
        o_ref[...] = (acc_ref[...] * gs_ref[...]).astype(o_ref.dtype)


def expert_ffn(xs, gs, w1, w3, w2, layer):
    e, r, d = xs.shape
    ff = w1.shape[3]
    tf = FFN_TF
    return pl.pallas_call(
        _ffn_kernel,
        grid=(e, ff // tf),
        in_specs=[pl.BlockSpec((None, r, d), lambda i, j: (i, 0, 0)),
                  pl.BlockSpec((None, r, 1), lambda i, j: (i, 0, 0)),
                  pl.BlockSpec((None, None, d, tf), lambda i, j: (layer, i, 0, j)),
                  pl.BlockSpec((None, None, d, tf), lambda i, j: (layer, i, 0, j)),
                  pl.BlockSpec((None, None, tf, d), lambda i, j: (layer, i, j, 0))],
        out_specs=pl.BlockSpec((None, r, d), lambda i, j: (i, 0, 0)),
        out_shape=jax.ShapeDtypeStruct((e, r, d), jnp.bfloat16),
        scratch_shapes=[pltpu.VMEM((r, d), jnp.float32)],
        compiler_params=pltpu.CompilerParams(
            dimension_semantics=("parallel", "arbitrary"), vmem_limit_bytes=VMEM_LIMIT),
        name="expert_ffn",
    )(xs, gs, w1, w3, w2)


HG_T = 64
HG_UNROLL = 2


def _hgrn_gate(raw, log_lb, log1m_lb, one_m_lb):
    ls = jnp.minimum(raw, 0.0) - jnp.log1p(jnp.exp(-jnp.abs(raw)))
    bb = log1m_lb + ls
    mx = jnp.maximum(log_lb, bb)
    logf = mx + jnp.log1p(jnp.exp(-jnp.abs(log_lb - bb)))
    return logf, one_m_lb * jax.nn.sigmoid(-raw)


HG_S = 8


def _hgrn_block(q, k, lf, v, st_ref, backward):
    t, s8 = HG_T, HG_S
    hd = HGRN_HEAD_DIM
    bf = jnp.bfloat16
    row8 = lax.broadcasted_iota(jnp.int32, (s8, hd), 0)
    r8 = lax.broadcasted_iota(jnp.int32, (s8, t), 0)
    lt = lax.broadcasted_iota(jnp.int32, (s8, t), 1)
    sdiag, cloc = [], []
    for i in range(t // s8):
        rows = slice(i * s8, (i + 1) * s8)
        qh, kh_, lfh = q[rows], k[rows], lf[rows]
        w = jnp.zeros_like(qh)
        sc = jnp.zeros((s8, t), jnp.float32)
        for d in range(s8):
            sh = (s8 - d) % s8 if backward else d
            kd = kh_ if d == 0 else pltpu.roll(kh_, sh, axis=0)
            lfd = lfh if d == 0 else pltpu.roll(lfh, sh, axis=0)
            valid = (row8 + d <= s8 - 1) if backward else (row8 >= d)
            p = jnp.where(valid, qh * kd * jnp.exp(w), 0.0)
            col = (r8 + d if backward else r8 - d) + i * s8
            sc = jnp.where(lt == col, jnp.sum(p, axis=-1, keepdims=True), sc)
            w = w + lfd
        sdiag.append(sc)
        c = lfh
        s = 1
        while s < s8:
            if backward:
                c = c + jnp.where(row8 + s <= s8 - 1, pltpu.roll(c, s8 - s, axis=0), 0.0)
            else:
                c = c + jnp.where(row8 >= s, pltpu.roll(c, s, axis=0), 0.0)
            s *= 2
        cloc.append(c)
    scores = jnp.concatenate(sdiag, axis=0)
    cl = jnp.concatenate(cloc, axis=0)
    row = lax.broadcasted_iota(jnp.int32, (t, hd), 0)
    ti = lax.broadcasted_iota(jnp.int32, (t, t), 0)
    tj = lax.broadcasted_iota(jnp.int32, (t, t), 1)
    h = s8
    while h < t:
        is_far = ((row // h) % 2 == 0) if backward else ((row // h) % 2 == 1)
        parts = []
        for blk in range(t // (2 * h)):
            e = blk * 2 * h + (h if backward else h - 1)
            parts.append(jnp.broadcast_to(cl[e:e + 1, :], (2 * h, hd)))
        tn = jnp.concatenate(parts, axis=0) if len(parts) > 1 else parts[0]
        qt = jnp.where(is_far, q * jnp.exp(cl), 0.0).astype(bf)
        kt = jnp.where(is_far, 0.0, k * jnp.exp(jnp.where(is_far, 0.0, tn - cl))).astype(bf)
        s_h = lax.dot_general(qt, kt, NT_DIMS, preferred_element_type=jnp.float32)
        if 2 * h < t:
            s_h = jnp.where(ti // (2 * h) == tj // (2 * h), s_h, 0.0)
        scores = scores + s_h
        cl = cl + jnp.where(is_far, tn, 0.0)
        h *= 2
    edge = 0 if backward else t - 1
    tot = cl[edge:edge + 1, :]
    st = st_ref[...]
    vb = v.astype(bf)
    o = jnp.dot(scores.astype(bf), vb, preferred_element_type=jnp.float32)
    o = o + lax.dot_general((q * jnp.exp(cl)).astype(bf), st.astype(bf), NT_DIMS,
                            preferred_element_type=jnp.float32)
    kh = (k * jnp.exp(tot - cl)).astype(bf)
    ut = lax.dot_general(vb, kh, (((0,), (0,)), ((), ())), preferred_element_type=jnp.float32)
    st_ref[...] = st * jnp.exp(tot) + ut
    return o


def _hgrn_kernel(q_ref, ff_ref, fb_ref, i_ref, g_ref, lb_ref, nw_ref, o_ref,
                 lff_ref, kf_ref, lfb_ref, kb_ref, of_ref, ob_ref, stf_ref, stb_ref):
    lb = lb_ref[...]
    log_lb = jnp.log(lb)
    log1m_lb = jnp.log1p(-lb)
    one_m_lb = 1.0 - lb
    lf, kk = _hgrn_gate(ff_ref[...], log_lb, log1m_lb, one_m_lb)
    lff_ref[...] = lf
    kf_ref[...] = kk
    lf, kk = _hgrn_gate(fb_ref[...], log_lb, log1m_lb, one_m_lb)
    lfb_ref[...] = lf
    kb_ref[...] = kk
    stf_ref[...] = jnp.zeros_like(stf_ref)
    stb_ref[...] = jnp.zeros_like(stb_ref)
    nblk = q_ref.shape[0] // HG_T

    def body(n, carry):
        for u in range(HG_UNROLL):
            blk = n * HG_UNROLL + u
            rf = pl.ds(pl.multiple_of(blk * HG_T, HG_T), HG_T)
            rb = pl.ds(pl.multiple_of((nblk - 1 - blk) * HG_T, HG_T), HG_T)
            of_ref[rf, :] = _hgrn_block(q_ref[rf, :], kf_ref[rf, :], lff_ref[rf, :], i_ref[rf, :], stf_ref, False)
            ob_ref[rb, :] = _hgrn_block(q_ref[rb, :], kb_ref[rb, :], lfb_ref[rb, :], i_ref[rb, :], stb_ref, True)
        return carry

    lax.fori_loop(0, nblk // HG_UNROLL, body, 0)
    o = of_ref[...] + ob_ref[...]
    o = o * lax.rsqrt(jnp.mean(o * o, axis=-1, keepdims=True) + NORM_EPS) * nw_ref[...]
    o_ref[...] = o * jax.nn.sigmoid(g_ref[...])


def hgrn2(proj, lb, norm_w, bsz, seq):
    m = proj.shape[0]
    hd = HGRN_HEAD_DIM
    nh = HGRN_HEADS

    def col(base):
        return pl.BlockSpec((seq, hd), lambda b, h: (b, base * nh + h))

    vec = pl.BlockSpec((1, hd), lambda b, h: (0, h))
    f32 = jnp.float32
    return pl.pallas_call(
        _hgrn_kernel,
        grid=(bsz, nh),
        in_specs=[col(1), col(2), col(3), col(4), col(5), vec, vec],
        out_specs=pl.BlockSpec((seq, hd), lambda b, h: (b, h)),
        out_shape=jax.ShapeDtypeStruct((m, HGRN_INNER), f32),
        scratch_shapes=[pltpu.VMEM((seq, hd), f32) for _ in range(6)]
        + [pltpu.VMEM((hd, hd), f32) for _ in range(2)],
        compiler_params=pltpu.CompilerParams(
            dimension_semantics=("parallel", "parallel"), vmem_limit_bytes=VMEM_LIMIT),
        name="hgrn2",
    )(proj, proj, proj, proj, proj, lb.reshape(1, -1), norm_w.reshape(1, -1))


ATT_TQ = 256
LANES = 128


def _group_mean_sq(x, g_ref):
    x2 = x * x
    hi = x2.astype(jnp.bfloat16)
    lo = (x2 - hi.astype(jnp.float32)).astype(jnp.bfloat16)
    g = g_ref[...]
    return (jnp.dot(hi, g, preferred_element_type=jnp.float32)
            + jnp.dot(lo, g, preferred_element_type=jnp.float32))


def _norm_rope(x, g_ref, w, cos, sin_signed):
    n = x.shape[1]
    y = x * lax.rsqrt(_group_mean_sq(x, g_ref) + NORM_EPS) * w
    lane = lax.broadcasted_iota(jnp.int32, y.shape, 1)
    partner = jnp.where(lane % 2 == 0, pltpu.roll(y, n - 1, axis=1), pltpu.roll(y, 1, axis=1))
    return y * cos + partner * sin_signed


def _dup_halves(x):
    lane = lax.broadcasted_iota(jnp.int32, x.shape, 1)
    sw = pltpu.roll(x, ATT_HEAD_DIM, axis=1)
    lo = lane < ATT_HEAD_DIM
    return jnp.where(lo, x, sw), jnp.where(lo, sw, x)


def _gqa_kernel(q_ref, k_ref, v_ref, cq_ref, sq_ref, ck_ref, sk_ref, gq_ref, gk_ref, qw_ref, kw_ref,
                o_ref, kd_ref, vd_ref):
    @pl.when(pl.program_id(1) == 0)
    def _():
        kr = _norm_rope(k_ref[...], gk_ref, kw_ref[...], ck_ref[...], sk_ref[...])
        k0, k1 = _dup_halves(kr)
        kd_ref[0] = k0.astype(jnp.bfloat16)
        kd_ref[1] = k1.astype(jnp.bfloat16)
        v0, v1 = _dup_halves(v_ref[...])
        vd_ref[0] = v0.astype(jnp.bfloat16)
        vd_ref[1] = v1.astype(jnp.bfloat16)

    tq = q_ref.shape[0]
    reps = ATT_INNER // LANES
    cos = jnp.concatenate([cq_ref[...]] * reps, axis=1)
    sin = jnp.concatenate([sq_ref[...]] * reps, axis=1)
    qr = _norm_rope(q_ref[...], gq_ref, qw_ref[...], cos, sin) * (ATT_HEAD_DIM ** -0.5)
    lane = lax.broadcasted_iota(jnp.int32, (tq, LANES), 1)
    lo = lane < ATT_HEAD_DIM
    rep = ATT_Q_HEADS // ATT_KV_HEADS
    for pair in range(ATT_Q_HEADS // 2):
        grp = (2 * pair) // rep
        qp = qr[:, pair * LANES:(pair + 1) * LANES]
        q2 = jnp.concatenate([jnp.where(lo, qp, 0.0), jnp.where(lo, 0.0, qp)], axis=0).astype(jnp.bfloat16)
        s = lax.dot_general(q2, kd_ref[grp], (((1,), (1,)), ((), ())), preferred_element_type=jnp.float32)
        s = s - jnp.max(s, axis=-1, keepdims=True)
        p = jnp.exp(s)
        l = jnp.sum(p, axis=-1, keepdims=True)
        o2 = jnp.dot(p.astype(jnp.bfloat16), vd_ref[grp], preferred_element_type=jnp.float32) / l
        o_ref[:, pair * LANES:(pair + 1) * LANES] = jnp.where(lo, o2[:tq], o2[tq:])


def gqa(proj, cos_t, sin_t, gq, q_w, k_w, bsz, seq):
    m = proj.shape[0]
    tq = ATT_TQ
    nq = seq // tq
    kvw = ATT_KV_HEADS * ATT_HEAD_DIM
    full = lambda shape: pl.BlockSpec(shape, lambda b, i: (0, 0))
    return pl.pallas_call(
        _gqa_kernel,
        grid=(bsz, nq),
        in_specs=[pl.BlockSpec((tq, ATT_INNER), lambda b, i: (b * nq + i, 1536 // ATT_INNER)),
                  pl.BlockSpec((seq, kvw), lambda b, i: (b, 2048 // kvw)),
                  pl.BlockSpec((seq, kvw), lambda b, i: (b, 2176 // kvw)),
                  pl.BlockSpec((tq, LANES), lambda b, i: (i, 0)),
                  pl.BlockSpec((tq, LANES), lambda b, i: (i, 0)),
                  full((seq, LANES)), full((seq, LANES)),
                  full((ATT_INNER, ATT_INNER)), full((LANES, LANES)),
                  full((1, ATT_INNER)), full((1, LANES))],
        out_specs=pl.BlockSpec((tq, ATT_INNER), lambda b, i: (b * nq + i, 0)),
        out_shape=jax.ShapeDtypeStruct((m, ATT_INNER), jnp.float32),
        scratch_shapes=[pltpu.VMEM((ATT_KV_HEADS, seq, LANES), jnp.bfloat16),
                        pltpu.VMEM((ATT_KV_HEADS, seq, LANES), jnp.bfloat16)],
        compiler_params=pltpu.CompilerParams(
            dimension_semantics=("parallel", "arbitrary"), vmem_limit_bytes=VMEM_LIMIT),
        name="gqa",
    )(proj, proj, proj, cos_t, sin_t, cos_t, sin_t, gq, gq[:LANES, :LANES],
      jnp.tile(q_w, ATT_Q_HEADS).reshape(1, -1), jnp.tile(k_w, ATT_KV_HEADS).reshape(1, -1))


def rope_lane_tables(seq):
    cos, sin = axial_rope_tables(seq)
    cos2 = jnp.repeat(cos, 2, axis=1)
    sin2 = jnp.repeat(sin, 2, axis=1) * jnp.tile(jnp.array([-1.0, 1.0], jnp.float32), ATT_HEAD_DIM // 2)
    return jnp.tile(cos2, (1, 2)), jnp.tile(sin2, (1, 2))


def head_mean_matrix():
    idx = jnp.arange(ATT_INNER) // ATT_HEAD_DIM
    return jnp.where(idx[:, None] == idx[None, :], 1.0 / ATT_HEAD_DIM, 0.0).astype(jnp.bfloat16)


NT_DIMS = (((1,), (1,)), ((), ()))
LN_TM = 512


def _layer_norm(x, g, b):
    mu = jnp.mean(x, axis=-1, keepdims=True)
    xc = x - mu
    var = jnp.mean(xc * xc, axis=-1, keepdims=True)
    return xc * lax.rsqrt(var + LN_EPS) * g + b


def _bf16_split(x):
    hi = x.astype(jnp.bfloat16)
    return hi, (x - hi.astype(jnp.float32)).astype(jnp.bfloat16)


def _mix_ln_router_kernel(ya_ref, yb_ref, x_ref, nw_ref, wa_ref, wb_ref, g_ref, b_ref, rwh_ref, rwl_ref,
                          x1_ref, x1b_ref, aff_ref, *, rms_ya):
    ya = ya_ref[...]
    if rms_ya:
        ya = ya * lax.rsqrt(jnp.mean(ya * ya, axis=-1, keepdims=True) + NORM_EPS) * nw_ref[...]
    mix = (jnp.dot(ya.astype(jnp.bfloat16), wa_ref[...], preferred_element_type=jnp.float32)
           + jnp.dot(yb_ref[...].astype(jnp.bfloat16), wb_ref[...], preferred_element_type=jnp.float32))
    x1 = _layer_norm(DEEPNORM_ALPHA * x_ref[...] + mix, g_ref[...], b_ref[...])
    x1_ref[...] = x1
    x1b_ref[...] = x1.astype(jnp.bfloat16)
    hi, lo = _bf16_split(x1)
    rwh = rwh_ref[...]
    lt = (lax.dot_general(rwh, hi, NT_DIMS, preferred_element_type=jnp.float32)
          + lax.dot_general(rwl_ref[...], hi, NT_DIMS, preferred_element_type=jnp.float32)
          + lax.dot_general(rwh, lo, NT_DIMS, preferred_element_type=jnp.float32))
    e = jnp.exp(lt - jnp.max(lt, axis=0, keepdims=True))
    aff_ref[0] = e / jnp.sum(e, axis=0, keepdims=True)


def mix_ln_router(ya, yb, x, w_out, ln_g, ln_b, router_w, bsz, seq, ya_norm_w=None):
    m, d = x.shape
    tm = LN_TM
    nt = seq // tm
    ka = ya.shape[1]
    rms_ya = ya_norm_w is not None
    nw = (ya_norm_w if rms_ya else jnp.ones((ka,), jnp.float32)).reshape(1, ka)
    wa = w_out[:ka].astype(jnp.bfloat16)
    wb = w_out[ka:].astype(jnp.bfloat16)
    rwh, rwl = _bf16_split(router_w.T)
    ne = router_w.shape[1]
    row = lambda w: pl.BlockSpec((tm, w), lambda i: (i, 0))
    full = lambda a: pl.BlockSpec(a.shape, lambda i: (0, 0))
    g2, b2 = ln_g.reshape(1, d), ln_b.reshape(1, d)
    return pl.pallas_call(
        functools.partial(_mix_ln_router_kernel, rms_ya=rms_ya),
        grid=(m // tm,),
        in_specs=[row(ka), row(yb.shape[1]), row(d), full(nw), full(wa), full(wb), full(g2), full(b2),
                  full(rwh), full(rwl)],
        out_specs=[row(d), row(d), pl.BlockSpec((1, ne, tm), lambda i: (i // nt, 0, i % nt))],
        out_shape=[jax.ShapeDtypeStruct((m, d), jnp.float32),
                   jax.ShapeDtypeStruct((m, d), jnp.bfloat16),
                   jax.ShapeDtypeStruct((bsz, ne, seq), jnp.float32)],
        compiler_params=pltpu.CompilerParams(
            dimension_semantics=("parallel",), vmem_limit_bytes=VMEM_LIMIT),
        name="mix_ln_router",
    )(ya, yb, x, nw, wa, wb, g2, b2, rwh, rwl)


ROUTE_BLK = 256


def _excl_cumsum_lanes(mask, tri):
    n = mask.shape[1]
    blk = tri.shape[0]
    run = jnp.zeros((mask.shape[0], 1), jnp.float32)
    outs = []
    for k in range(n // blk):
        mb = mask[:, k * blk:(k + 1) * blk]
        outs.append(jnp.dot(mb.astype(jnp.bfloat16), tri, preferred_element_type=jnp.float32) + run)
        run = run + jnp.sum(mb, axis=-1, keepdims=True)
    return jnp.concatenate(outs, axis=1)


def _route_kernel(aff_ref, tri_ref, pos_ref, gsel_ref, *, cap):
    a = aff_ref[0]
    capf = jnp.float32(cap)
    bits = jnp.zeros((a.shape[0], 1), jnp.int32)
    for bit in range(30, -1, -1):
        cand = bits | (1 << bit)
        cnt = jnp.sum(jnp.where(a >= pltpu.bitcast(cand, jnp.float32), 1.0, 0.0), axis=-1, keepdims=True)
        bits = jnp.where(cnt >= capf, cand, bits)
    thr = pltpu.bitcast(bits, jnp.float32)
    gt = jnp.where(a > thr, 1.0, 0.0)
    eq = jnp.where(a == thr, 1.0, 0.0)
    need = capf - jnp.sum(gt, axis=-1, keepdims=True)
    tri = tri_ref[...]
    sel = gt + eq * jnp.where(_excl_cumsum_lanes(eq, tri) < need, 1.0, 0.0)
    pos = _excl_cumsum_lanes(sel, tri)
    chosen = sel > 0.5
    pos_ref[0] = jnp.where(chosen, pos, -1.0)
    gsel_ref[0] = jnp.where(chosen, a, 0.0)


def route(aff, cap):
    bsz, ne, seq = aff.shape
    blk = min(ROUTE_BLK, seq)
    idx = jnp.arange(blk)
    tri = (idx[:, None] < idx[None, :]).astype(jnp.bfloat16)
    spec = pl.BlockSpec((1, ne, seq), lambda b: (b, 0, 0))
    return pl.pallas_call(
        functools.partial(_route_kernel, cap=cap),
        grid=(bsz,),
        in_specs=[spec, pl.BlockSpec((blk, blk), lambda b: (0, 0))],
        out_specs=[spec, spec],
        out_shape=[jax.ShapeDtypeStruct((bsz, ne, seq), jnp.float32),
                   jax.ShapeDtypeStruct((bsz, ne, seq), jnp.float32)],
        compiler_params=pltpu.CompilerParams(
            dimension_semantics=("parallel",), vmem_limit_bytes=VMEM_LIMIT),
        name="route",
    )(aff, tri)


GATHER_NE = 4


def _gather_kernel(pos_ref, gsel_ref, x_ref, o_ref, gs_ref):
    nge, cap, d = o_ref.shape
    seq = x_ref.shape[0]
    slot = lax.broadcasted_iota(jnp.int32, (cap, seq), 0).astype(jnp.float32)
    hots = []
    for e in range(nge):
        hit = pos_ref[0, e] == slot
        hots.append(jnp.where(hit, 1.0, 0.0).astype(jnp.bfloat16))
        gs_ref[e] = jnp.sum(jnp.where(hit, gsel_ref[0, e], 0.0), axis=-1, keepdims=True)
    rows = jnp.dot(jnp.concatenate(hots, axis=0), x_ref[...], preferred_element_type=jnp.float32)
    o_ref[...] = rows.astype(jnp.bfloat16).reshape(nge, cap, d)


def gather_tokens(pos, gsel, x1b, cap):
    bsz, ne, seq = pos.shape
    d = x1b.shape[1]
    nge = GATHER_NE
    sel = pl.BlockSpec((1, nge, 1, seq), lambda b, g: (b, g, 0, 0))
    return pl.pallas_call(
        _gather_kernel,
        grid=(bsz, ne // nge),
        in_specs=[sel, sel, pl.BlockSpec((seq, d), lambda b, g: (b, 0))],
        out_specs=[pl.BlockSpec((nge, cap, d), lambda b, g: (g, b, 0)),
                   pl.BlockSpec((nge, cap, 1), lambda b, g: (g, b, 0))],
        out_shape=[jax.ShapeDtypeStruct((ne, bsz * cap, d), jnp.bfloat16),
                   jax.ShapeDtypeStruct((ne, bsz * cap, 1), jnp.float32)],
        compiler_params=pltpu.CompilerParams(
            dimension_semantics=("parallel", "parallel"), vmem_limit_bytes=VMEM_LIMIT),
        name="gather_tokens",
    )(pos.reshape(bsz, ne, 1, seq), gsel.reshape(bsz, ne, 1, seq), x1b)


COMB_TR = 512


def _combine_ln_kernel(ys_ref, post_ref, x1_ref, g_ref, b_ref, o_ref):
    post = post_ref[0]
    ne, cap, d = ys_ref.shape
    lane = lax.broadcasted_iota(jnp.int32, (post.shape[0], cap), 1).astype(jnp.float32)
    hots = [jnp.where(post[:, e:e + 1] == lane, 1.0, 0.0).astype(jnp.bfloat16) for e in range(ne)]
    ffn = jnp.dot(jnp.concatenate(hots, axis=1), ys_ref[...].reshape(ne * cap, d),
                  preferred_element_type=jnp.float32)
    o_ref[...] = _layer_norm(DEEPNORM_ALPHA * x1_ref[...] + ffn, g_ref[...], b_ref[...])


def combine_ln(ys, post, x1, ln_g, ln_b, cap):
    bsz, seq, ne = post.shape
    m, d = x1.shape
    tr = min(COMB_TR, seq)
    nr = seq // tr
    g2, b2 = ln_g.reshape(1, d), ln_b.reshape(1, d)
    vec = pl.BlockSpec((1, d), lambda b, r: (0, 0))
    rows = pl.BlockSpec((tr, d), lambda b, r: (b * nr + r, 0))
    return pl.pallas_call(
        _combine_ln_kernel,
        grid=(bsz, nr),
        in_specs=[pl.BlockSpec((ne, cap, d), lambda b, r: (0, b, 0)),
                  pl.BlockSpec((1, tr, ne), lambda b, r: (b, r, 0)), rows, vec, vec],
        out_specs=rows,
        out_shape=jax.ShapeDtypeStruct((m, d), jnp.float32),
        compiler_params=pltpu.CompilerParams(
            dimension_semantics=("parallel", "parallel"), vmem_limit_bytes=VMEM_LIMIT),
        name="combine_ln",
    )(ys, post, x1, g2, b2)


def moe_block(ya, yb, x2d, w_out, ln1_g, ln1_b, router_w, w1, w3, w2, layer, ln2_g, ln2_b, bsz, seq,
              ya_norm_w=None):
    cap = CAPACITY_FACTOR * seq // N_EXPERTS
    x1, x1b, aff = mix_ln_router(ya, yb, x2d, w_out, ln1_g, ln1_b, router_w, bsz, seq, ya_norm_w)
    pos, gsel = route(aff, cap)
    xs, gs = gather_tokens(pos, gsel, x1b, cap)
    ys = expert_ffn(xs, gs, w1, w3, w2, layer)
    return combine_ln(ys, pos.transpose(0, 2, 1), x1, ln2_g, ln2_b, cap)


SSD_GH = SSD_HEADS // SSD_GROUPS
SSD_GW = SSD_GH * SSD_HEAD_DIM


def _shift_rows(x, s, row):
    n = x.shape[0]
    if s == 0:
        return x
    y = pltpu.roll(x, (-s) % n, axis=0)
    return jnp.where(row >= -s, y, 0.0) if s < 0 else jnp.where(row < n - s, y, 0.0)


def _conv_silu(x, w_ref, b_ref):
    row = lax.broadcasted_iota(jnp.int32, x.shape, 0)
    pad = (SSD_CONV_K - 1) // 2
    y = b_ref[...] + jnp.zeros_like(x)
    for k in range(SSD_CONV_K):
        y = y + w_ref[k:k + 1, :] * _shift_rows(x, k - pad, row)
    return y * jax.nn.sigmoid(y)


def _cumsum_rows(x, row, reverse):
    n = x.shape[0]
    s = 1
    while s < n:
        if reverse:
            x = x + jnp.where(row + s < n, pltpu.roll(x, n - s, axis=0), 0.0)
        else:
            x = x + jnp.where(row >= s, pltpu.roll(x, s, axis=0), 0.0)
        s *= 2
    return x


def _expand_heads(v, cols, lane_lo):
    parts = []
    for p in range(len(cols) // 2):
        a = v[:, cols[2 * p]:cols[2 * p] + 1]
        b = v[:, cols[2 * p + 1]:cols[2 * p + 1] + 1]
        parts.append(jnp.where(lane_lo, a, b))
    return jnp.concatenate(parts, axis=1)


def _ssd_chunk(ux, ub, uc, dtv, la, st_ref, reverse):
    q = SSD_CHUNK
    bf = jnp.bfloat16
    cols = [SSD_GH + h if reverse else h for h in range(SSD_GH)]
    row = lax.broadcasted_iota(jnp.int32, (q, LANES), 0)
    lane = lax.broadcasted_iota(jnp.int32, (q, LANES), 1)
    lane_lo = lane < SSD_HEAD_DIM
    tri = (row <= lane) if reverse else (row >= lane)
    cum = _cumsum_rows(la, row, reverse)
    cum_t = cum.T
    tot = cum[0:1, :] if reverse else cum[q - 1:q, :]
    ub16 = ub.astype(bf)
    uc16 = uc.astype(bf)
    cb = lax.dot_general(uc16, ub16, NT_DIMS, preferred_element_type=jnp.float32)
    xdt = ux * _expand_heads(dtv, cols, lane_lo)
    ys = []
    for p in range(SSD_GH // 2):
        xp = xdt[:, p * LANES:(p + 1) * LANES]
        acc = None
        for half in range(2):
            c = cols[2 * p + half]
            seg = cum[:, c:c + 1] - cum_t[c:c + 1, :]
            lm = jnp.where(tri, jnp.exp(seg), 0.0)
            xh = jnp.where(lane_lo, xp, 0.0) if half == 0 else jnp.where(lane_lo, 0.0, xp)
            t = jnp.dot((cb * lm).astype(bf), xh.astype(bf), preferred_element_type=jnp.float32)
            acc = t if acc is None else acc + t
        ys.append(acc)
    y = jnp.concatenate(ys, axis=1)
    st = st_ref[...]
    y = y + _expand_heads(jnp.exp(cum), cols, lane_lo) * jnp.dot(uc16, st.astype(bf), preferred_element_type=jnp.float32)
    xe = (xdt * _expand_heads(jnp.exp(tot - cum), cols, lane_lo)).astype(bf)
    new = lax.dot_general(ub16, xe, (((0,), (0,)), ((), ())), preferred_element_type=jnp.float32)
    st_ref[...] = st * _expand_heads(jnp.exp(tot), cols, lane_lo[0:1]) + new
    return y


def _ssd_kernel(z_ref, x_ref, b_ref, c_ref, dt_ref, cwx_ref, cwb_ref, cwc_ref, cbx_ref, cbb_ref, cbc_ref,
                dtb_ref, aneg_ref, dsk_ref, o_ref, ux_ref, ub_ref, uc_ref, dtv_ref, la_ref, yf_ref, yb_ref,
                stf_ref, stb_ref):
    ux_ref[...] = _conv_silu(x_ref[...], cwx_ref, cbx_ref)
    ub_ref[...] = _conv_silu(b_ref[...], cwb_ref, cbb_ref)
    uc_ref[...] = _conv_silu(c_ref[...], cwc_ref, cbc_ref)
    t = dt_ref[...] + dtb_ref[...]
    dtv = jnp.maximum(t, 0.0) + jnp.log1p(jnp.exp(-jnp.abs(t)))
    dtv_ref[...] = dtv
    la_ref[...] = dtv * aneg_ref[...]
    stf_ref[...] = jnp.zeros_like(stf_ref)
    stb_ref[...] = jnp.zeros_like(stb_ref)
    q = SSD_CHUNK
    nchunk = x_ref.shape[0] // q

    def body(c, carry):
        rf = pl.ds(pl.multiple_of(c * q, q), q)
        rb = pl.ds(pl.multiple_of((nchunk - 1 - c) * q, q), q)
        yf_ref[rf, :] = _ssd_chunk(ux_ref[rf, :], ub_ref[rf, :], uc_ref[rf, :], dtv_ref[rf, :], la_ref[rf, :], stf_ref, False)
        yb_ref[rb, :] = _ssd_chunk(ux_ref[rb, :], ub_ref[rb, :], uc_ref[rb, :], dtv_ref[rb, :], la_ref[rb, :], stb_ref, True)
        return carry

    lax.fori_loop(0, nchunk, body, 0)
    z = z_ref[...]
    y = yf_ref[...] + yb_ref[...] + ux_ref[...] * dsk_ref[...]
    o_ref[...] = y * (z * jax.nn.sigmoid(z))


def ssd(proj, dt, conv_w, conv_b, dt_bias_g, a_neg_g, d_skip_g, bsz, seq):
    m = proj.shape[0]
    gw, n = SSD_GW, SSD_STATE
    f32 = jnp.float32
    kk = SSD_CONV_K
    cb2 = conv_b.reshape(1, -1)
    in_specs = [
        pl.BlockSpec((seq, gw), lambda b, g: (b, g)),
        pl.BlockSpec((seq, gw), lambda b, g: (b, SSD_INNER // gw + g)),
        pl.BlockSpec((seq, n), lambda b, g: (b, 2 * SSD_INNER // n + g)),
        pl.BlockSpec((seq, n), lambda b, g: (b, 2 * SSD_INNER // n + SSD_GROUPS + g)),
        pl.BlockSpec((seq, LANES), lambda b, g: (b, g)),
        pl.BlockSpec((kk, gw), lambda b, g: (0, g)),
        pl.BlockSpec((kk, n), lambda b, g: (0, SSD_INNER // n + g)),
        pl.BlockSpec((kk, n), lambda b, g: (0, SSD_INNER // n + SSD_GROUPS + g)),
        pl.BlockSpec((1, gw), lambda b, g: (0, g)),
        pl.BlockSpec((1, n), lambda b, g: (0, SSD_INNER // n + g)),
        pl.BlockSpec((1, n), lambda b, g: (0, SSD_INNER // n + SSD_GROUPS + g)),
        pl.BlockSpec((1, LANES), lambda b, g: (0, g)),
        pl.BlockSpec((1, LANES), lambda b, g: (0, g)),
        pl.BlockSpec((1, gw), lambda b, g: (0, g)),
    ]
    return pl.pallas_call(
        _ssd_kernel,
        grid=(bsz, SSD_GROUPS),
        in_specs=in_specs,
        out_specs=pl.BlockSpec((seq, gw), lambda b, g: (b, g)),
        out_shape=jax.ShapeDtypeStruct((m, SSD_INNER), f32),
        scratch_shapes=[pltpu.VMEM((seq, gw), f32), pltpu.VMEM((seq, n), f32), pltpu.VMEM((seq, n), f32),
                        pltpu.VMEM((seq, LANES), f32), pltpu.VMEM((seq, LANES), f32),
                        pltpu.VMEM((seq, gw), f32), pltpu.VMEM((seq, gw), f32),
                        pltpu.VMEM((n, gw), f32), pltpu.VMEM((n, gw), f32)],
        compiler_params=pltpu.CompilerParams(
            dimension_semantics=("parallel", "parallel"), vmem_limit_bytes=VMEM_LIMIT),
        name="ssd",
    )(proj, proj, proj, proj, dt, conv_w, conv_w, conv_w, cb2, cb2, cb2, dt_bias_g, a_neg_g, d_skip_g)


def _group_lanes(v):
    out = jnp.zeros((SSD_GROUPS, LANES), v.dtype)
    for g in range(SSD_GROUPS):
        out = out.at[g, :SSD_GH].set(v[0, g * SSD_GH:(g + 1) * SSD_GH])
        out = out.at[g, SSD_GH:2 * SSD_GH].set(v[1, g * SSD_GH:(g + 1) * SSD_GH])
    return out.reshape(1, -1)


def _dt_weight(w_dt):
    d = w_dt.shape[0]
    out = jnp.zeros((d, SSD_GROUPS, LANES), w_dt.dtype)
    for g in range(SSD_GROUPS):
        out = out.at[:, g, :SSD_GH].set(w_dt[:, g * SSD_GH:(g + 1) * SSD_GH])
        out = out.at[:, g, SSD_GH:2 * SSD_GH].set(w_dt[:, SSD_HEADS + g * SSD_GH:SSD_HEADS + (g + 1) * SSD_GH])
    return out.reshape(d, -1)


def _dt_proj_kernel(x_ref, wh_ref, wl_ref, o_ref):
    hi, lo = _bf16_split(x_ref[...])
    wh = wh_ref[...]
    o_ref[...] = (jnp.dot(hi, wh, preferred_element_type=jnp.float32)
                  + jnp.dot(hi, wl_ref[...], preferred_element_type=jnp.float32)
                  + jnp.dot(lo, wh, preferred_element_type=jnp.float32))


def dt_proj(x, w_dt_g):
    m, k = x.shape
    n = w_dt_g.shape[1]
    wh, wl = _bf16_split(w_dt_g)
    tm = LN_TM
    return pl.pallas_call(
        _dt_proj_kernel,
        grid=(m // tm,),
        in_specs=[pl.BlockSpec((tm, k), lambda i: (i, 0)),
                  pl.BlockSpec((k, n), lambda i: (0, 0)), pl.BlockSpec((k, n), lambda i: (0, 0))],
        out_specs=pl.BlockSpec((tm, n), lambda i: (i, 0)),
        out_shape=jax.ShapeDtypeStruct((m, n), jnp.float32),
        compiler_params=pltpu.CompilerParams(
            dimension_semantics=("parallel",), vmem_limit_bytes=VMEM_LIMIT),
        name="dt_proj",
    )(x, wh, wl)


def _pool_kernel(u_ref, w_ref, sc_ref, o_ref):
    n = u_ref.shape[0]
    row = lax.broadcasted_iota(jnp.int32, (n, POOL_GROUP), 0)
    for gi, win in enumerate(POOL_WINDOWS):
        u = u_ref[:, gi * POOL_GROUP:(gi + 1) * POOL_GROUP]
        half = win // 2
        fwd, bwd = u, u
        s = 1
        while s < half:
            fwd = fwd + _shift_rows(fwd, s, row)
            bwd = bwd + _shift_rows(bwd, -s, row)
            s *= 2
        total = fwd + _shift_rows(bwd, -1, row)
        cnt = (jnp.minimum(row + half, n) - jnp.maximum(row - half, 0)).astype(jnp.float32)
        pooled = total / cnt - u
        mixed = jnp.dot(pooled.astype(jnp.bfloat16), w_ref[gi], preferred_element_type=jnp.float32)
        o_ref[:, gi * POOL_GROUP:(gi + 1) * POOL_GROUP] = mixed * sc_ref[:, gi * POOL_GROUP:(gi + 1) * POOL_GROUP]


def multiscale_pool_kernel(proj, pool_w, pool_scale, bsz, seq):
    m = proj.shape[0]
    return pl.pallas_call(
        _pool_kernel,
        grid=(bsz,),
        in_specs=[pl.BlockSpec((seq, POOL_INNER), lambda b: (b, 0)),
                  pl.BlockSpec(pool_w.shape, lambda b: (0, 0, 0)),
                  pl.BlockSpec((1, POOL_INNER), lambda b: (0, 0))],
        out_specs=pl.BlockSpec((seq, POOL_INNER), lambda b: (b, 0)),
        out_shape=jax.ShapeDtypeStruct((m, POOL_INNER), jnp.float32),
        compiler_params=pltpu.CompilerParams(
            dimension_semantics=("parallel",), vmem_limit_bytes=VMEM_LIMIT),
        name="multiscale_pool",
    )(proj, pool_w.astype(jnp.bfloat16), pool_scale.reshape(1, -1))


def axial_rope_tables(seq):
    rows = seq // GRID_W
    row = jnp.repeat(jnp.arange(rows), GRID_W).astype(jnp.float32)
    col = jnp.tile(jnp.arange(GRID_W), rows).astype(jnp.float32)
    axis_dims = ATT_HEAD_DIM // 2
    freqs = ROPE_THETA ** (-jnp.arange(0, axis_dims, 2, dtype=jnp.float32) / axis_dims)
    ang = jnp.concatenate([row[:, None] * freqs, col[:, None] * freqs], axis=-1)
    return jnp.cos(ang), jnp.sin(ang)


def kernel(x, w_in_ab, ssm_conv_w, ssm_conv_b, ssm_dt_bias, ssm_a_log, ssm_d, ssm_norm_w, attn_q_norm, attn_k_norm, w_out_ab, w_in_cd, pool_w, pool_scale, hgrn_lb_logits, hgrn_norm_w, w_out_cd, router_w, moe_w1, moe_w3, moe_w2, ln1_g, ln1_b, ln2_g, ln2_b):
    bsz, seq, d = x.shape
    m = bsz * seq
    cos_t, sin_t = rope_lane_tables(seq)
    gq = head_mean_matrix()
    lb_all = jnp.cumsum(jax.nn.softmax(hgrn_lb_logits, axis=0), axis=0)
    lb_all = lb_all - lb_all[0]
    x2 = x.reshape(m, d)
    for layer in range(DEPTH):
        j = layer // 2
        if layer % 2 == 0:
            w = w_in_ab[j]
            w_main = jnp.concatenate([w[:, :1536], w[:, 1552:]], axis=1).astype(jnp.bfloat16)
            proj = matmul(x2, w_main)
            dt = dt_proj(x2, _dt_weight(w[:, 1536:1552]))
            ya = ssd(proj, dt, ssm_conv_w[j], ssm_conv_b[j], _group_lanes(ssm_dt_bias[j]),
                     _group_lanes(-jnp.exp(ssm_a_log[j])), jnp.repeat(ssm_d[j], SSD_HEAD_DIM).reshape(1, -1),
                     bsz, seq)
            yb = gqa(proj, cos_t, sin_t, gq, attn_q_norm[j], attn_k_norm[j], bsz, seq)
            w_out, ya_norm_w = w_out_ab[j], ssm_norm_w[j]
        else:
            proj = matmul(x2, w_in_cd[j].astype(jnp.bfloat16))
            ya = multiscale_pool_kernel(proj, pool_w[j], pool_scale[j], bsz, seq)
            yb = hgrn2(proj, lb_all[layer], hgrn_norm_w[j], bsz, seq)
            w_out, ya_norm_w = w_out_cd[j], None
        x2 = moe_block(ya, yb, x2, w_out, ln1_g[layer], ln1_b[layer], router_w[layer],
                       moe_w1, moe_w3, moe_w2, layer, ln2_g[layer], ln2_b[layer], bsz, seq, ya_norm_w)
    return x2.reshape(bsz, seq, d)
```

```python
import functools
import math

import jax
import jax.numpy as jnp
from jax import lax
from jax.experimental import pallas as pl
from jax.experimental.pallas import tpu as pltpu

D_MODEL = 1024
DEPTH = 4
GRID_W = 64
NORM_EPS = 1e-6
LN_EPS = 1e-5

SSD_HEADS = 8
SSD_HEAD_DIM = 64
SSD_INNER = 512
SSD_GROUPS = 2
SSD_STATE = 128
SSD_CONV_K = 5
SSD_CHUNK = 128
SSD_XBC = 1024

ATT_Q_HEADS = 8
ATT_KV_HEADS = 2
ATT_HEAD_DIM = 64
ATT_INNER = 512
ATT_BLOCK = 128
ROPE_THETA = 10000.0

POOL_WINDOWS = (2, 4, 8, 16)
POOL_GROUP = 128
POOL_INNER = 512

HGRN_HEADS = 4
HGRN_HEAD_DIM = 128
HGRN_INNER = 512
HGRN_CHUNK = 64

N_EXPERTS = 16
EXPERT_FF = 2048
CAPACITY_FACTOR = 2

DEEPNORM_ALPHA = (2 * DEPTH) ** 0.25

AB_WIDTHS = (SSD_INNER, SSD_XBC, 2 * SSD_HEADS, ATT_INNER, ATT_KV_HEADS * ATT_HEAD_DIM, ATT_KV_HEADS * ATT_HEAD_DIM)
CD_WIDTHS = (POOL_INNER, HGRN_INNER, HGRN_INNER, HGRN_INNER, HGRN_INNER, HGRN_INNER)

VMEM_LIMIT = 56 * 1024 * 1024


def _mm_kernel(x_ref, w_ref, o_ref):
    o_ref[...] = jnp.dot(x_ref[...].astype(jnp.bfloat16), w_ref[...],
                         preferred_element_type=jnp.float32)


MM_TM = 2048
MM_TN = 768


def matmul(x, w_bf16, *, tm=MM_TM, tn=MM_TN):
    m, k = x.shape
    n = w_bf16.shape[1]
    return pl.pallas_call(
        _mm_kernel,
        grid=(m // tm, n // tn),
        in_specs=[pl.BlockSpec((tm, k), lambda i, j: (i, 0)),
                  pl.BlockSpec((k, tn), lambda i, j: (0, j))],
        out_specs=pl.BlockSpec((tm, tn), lambda i, j: (i, j)),
        out_shape=jax.ShapeDtypeStruct((m, n), jnp.float32),
        compiler_params=pltpu.CompilerParams(
            dimension_semantics=("parallel", "parallel"), vmem_limit_bytes=VMEM_LIMIT),
        name="matmul",
    )(x, w_bf16)


FFN_TF = 512


FFN_ROWS = 512


def _ffn_kernel(xs_ref, gs_ref, w1_ref, w3_ref, w2_ref, o_ref, acc_ref):
    f = pl.program_id(1)

    @pl.when(f == 0)
    def _():
        acc_ref[...] = jnp.zeros_like(acc_ref)

    w1 = w1_ref[...].astype(jnp.bfloat16)
    w3 = w3_ref[...].astype(jnp.bfloat16)
    w2 = w2_ref[...].astype(jnp.bfloat16)
    rc = min(FFN_ROWS, xs_ref.shape[0])
    for c in range(xs_ref.shape[0] // rc):
        rows = slice(c * rc, (c + 1) * rc)
        xc = xs_ref[rows, :]
        h1 = jnp.dot(xc, w1, preferred_element_type=jnp.float32)
        h3 = jnp.dot(xc, w3, preferred_element_type=jnp.float32)
        hdn = (h1 * jax.nn.sigmoid(h1) * h3).astype(jnp.bfloat16)
        acc_ref[rows, :] += jnp.dot(hdn, w2, preferred_element_type=jnp.float32)

    @pl.when(f == pl.num_programs(1) - 1)
    def _():
        o_ref[...] = (acc_ref[...] * gs_ref[...]).astype(o_ref.dtype)


def expert_ffn(xs, gs, w1, w3, w2, layer):
    e, r, d = xs.shape
    ff = w1.shape[3]
    tf = FFN_TF
    return pl.pallas_call(
        _ffn_kernel,
        grid=(e, ff // tf),
        in_specs=[pl.BlockSpec((None, r, d), lambda i, j: (i, 0, 0)),
                  pl.BlockSpec((None, r, 1), lambda i, j: (i, 0, 0)),
                  pl.BlockSpec((None, None, d, tf), lambda i, j: (layer, i, 0, j)),
                  pl.BlockSpec((None, None, d, tf), lambda i, j: (layer, i, 0, j)),
                  pl.BlockSpec((None, None, tf, d), lambda i, j: (layer, i, j, 0))],
        out_specs=pl.BlockSpec((None, r, d), lambda i, j: (i, 0, 0)),
        out_shape=jax.ShapeDtypeStruct((e, r, d), jnp.bfloat16),
        scratch_shapes=[pltpu.VMEM((r, d), jnp.float32)],
        compiler_params=pltpu.CompilerParams(
            dimension_semantics=("parallel", "arbitrary"), vmem_limit_bytes=VMEM_LIMIT),
        name="expert_ffn",
    )(xs, gs, w1, w3, w2)


HG_T = 64
HG_UNROLL = 4


def _hgrn_gate(raw, log_lb, log1m_lb, one_m_lb):
    e = jnp.exp(-jnp.abs(raw))
    den = 1.0 + e
    ls = jnp.minimum(raw, 0.0) - jnp.log(den)
    bb = log1m_lb + ls
    mx = jnp.maximum(log_lb, bb)
    logf = mx + jnp.log(1.0 + jnp.exp(-jnp.abs(log_lb - bb)))
    sig_neg = jnp.where(raw >= 0.0, e, 1.0) / den
    return logf * math.log2(math.e), one_m_lb * sig_neg


HG_S = 8


def _hgrn_block(q, k, lf, v, st_ref, backward):
    t, s8 = HG_T, HG_S
    hd = HGRN_HEAD_DIM
    bf = jnp.bfloat16
    row8 = lax.broadcasted_iota(jnp.int32, (s8, hd), 0)
    r8 = lax.broadcasted_iota(jnp.int32, (s8, t), 0)
    lt = lax.broadcasted_iota(jnp.int32, (s8, t), 1)
    sdiag, cloc = [], []
    for i in range(t // s8):
        rows = slice(i * s8, (i + 1) * s8)
        qh, kh_, lfh = q[rows], k[rows], lf[rows]
        w = jnp.zeros_like(qh)
        sc = jnp.zeros((s8, t), jnp.float32)
        for d in range(s8):
            sh = (s8 - d) % s8 if backward else d
            kd = kh_ if d == 0 else pltpu.roll(kh_, sh, axis=0)
            lfd = lfh if d == 0 else pltpu.roll(lfh, sh, axis=0)
            p = qh * kd * jnp.exp2(w)
            valid = (r8 + d <= s8 - 1) if backward else (r8 >= d)
            col = jnp.where(valid, (r8 + d if backward else r8 - d) + i * s8, -1)
            sc = jnp.where(lt == col, jnp.sum(p, axis=-1, keepdims=True), sc)
            w = w + lfd
        sdiag.append(sc)
        c = lfh
        s = 1
        while s < s8:
            if backward:
                c = c + jnp.where(row8 + s <= s8 - 1, pltpu.roll(c, s8 - s, axis=0), 0.0)
            else:
                c = c + jnp.where(row8 >= s, pltpu.roll(c, s, axis=0), 0.0)
            s *= 2
        cloc.append(c)
    scores = jnp.concatenate(sdiag, axis=0)
    cl = jnp.concatenate(cloc, axis=0)
    row = lax.broadcasted_iota(jnp.int32, (t, hd), 0)
    ti = lax.broadcasted_iota(jnp.int32, (t, t), 0)
    tj = lax.broadcasted_iota(jnp.int32, (t, t), 1)
    h = s8
    while h < t:
        is_far = ((row // h) % 2 == 0) if backward else ((row // h) % 2 == 1)
        parts = []
        for blk in range(t // (2 * h)):
            e = blk * 2 * h + (h if backward else h - 1)
            parts.append(jnp.broadcast_to(cl[e:e + 1, :], (2 * h, hd)))
        tn = jnp.concatenate(parts, axis=0) if len(parts) > 1 else parts[0]
        qt = jnp.where(is_far, q * jnp.exp2(cl), 0.0).astype(bf)
        kt = jnp.where(is_far, 0.0, k * jnp.exp2(jnp.where(is_far, 0.0, tn - cl))).astype(bf)
        s_h = lax.dot_general(qt, kt, NT_DIMS, preferred_element_type=jnp.float32)
        if 2 * h < t:
            s_h = jnp.where(ti // (2 * h) == tj // (2 * h), s_h, 0.0)
        scores = scores + s_h
        cl = cl + jnp.where(is_far, tn, 0.0)
        h *= 2
    edge = 0 if backward else t - 1
    tot = cl[edge:edge + 1, :]
    st = st_ref[...]
    vb = v.astype(bf)
    o = jnp.dot(scores.astype(bf), vb, preferred_element_type=jnp.float32)
    o = o + lax.dot_general((q * jnp.exp2(cl)).astype(bf), st.astype(bf), NT_DIMS,
                            preferred_element_type=jnp.float32)
    kh = (k * jnp.exp2(tot - cl)).astype(bf)
    ut = lax.dot_general(vb, kh, (((0,), (0,)), ((), ())), preferred_element_type=jnp.float32)
    st_ref[...] = st * jnp.exp2(tot) + ut
    return o


def _hgrn_kernel(q_ref, ff_ref, fb_ref, i_ref, g_ref, lb_ref, nw_ref, o_ref,
                 lff_ref, kf_ref, lfb_ref, kb_ref, of_ref, ob_ref, stf_ref, stb_ref):
    lb = lb_ref[...]
    log_lb = jnp.log(lb)
    log1m_lb = jnp.log1p(-lb)
    one_m_lb = 1.0 - lb
    lf, kk = _hgrn_gate(ff_ref[...], log_lb, log1m_lb, one_m_lb)
    lff_ref[...] = lf
    kf_ref[...] = kk
    lf, kk = _hgrn_gate(fb_ref[...], log_lb, log1m_lb, one_m_lb)
    lfb_ref[...] = lf
    kb_ref[...] = kk
    stf_ref[...] = jnp.zeros_like(stf_ref)
    stb_ref[...] = jnp.zeros_like(stb_ref)
    nblk = q_ref.shape[0] // HG_T

    def body(n, carry):
        for u in range(HG_UNROLL):
            blk = n * HG_UNROLL + u
            rf = pl.ds(pl.multiple_of(blk * HG_T, HG_T), HG_T)
            rb = pl.ds(pl.multiple_of((nblk - 1 - blk) * HG_T, HG_T), HG_T)
            of_ref[rf, :] = _hgrn_block(q_ref[rf, :], kf_ref[rf, :], lff_ref[rf, :], i_ref[rf, :], stf_ref, False)
            ob_ref[rb, :] = _hgrn_block(q_ref[rb, :], kb_ref[rb, :], lfb_ref[rb, :], i_ref[rb, :], stb_ref, True)
        return carry

    lax.fori_loop(0, nblk // HG_UNROLL, body, 0)
    o = of_ref[...] + ob_ref[...]
    o = o * lax.rsqrt(jnp.mean(o * o, axis=-1, keepdims=True) + NORM_EPS) * nw_ref[...]
    o_ref[...] = o * jax.nn.sigmoid(g_ref[...])


def hgrn2(proj, lb, norm_w, bsz, seq):
    m = proj.shape[0]
    hd = HGRN_HEAD_DIM
    nh = HGRN_HEADS

    def col(base):
        return pl.BlockSpec((seq, hd), lambda b, h: (b, base * nh + h))

    vec = pl.BlockSpec((1, hd), lambda b, h: (0, h))
    f32 = jnp.float32
    return pl.pallas_call(
        _hgrn_kernel,
        grid=(bsz, nh),
        in_specs=[col(1), col(2), col(3), col(4), col(5), vec, vec],
        out_specs=pl.BlockSpec((seq, hd), lambda b, h: (b, h)),
        out_shape=jax.ShapeDtypeStruct((m, HGRN_INNER), f32),
        scratch_shapes=[pltpu.VMEM((seq, hd), f32) for _ in range(6)]
        + [pltpu.VMEM((hd, hd), f32) for _ in range(2)],
        compiler_params=pltpu.CompilerParams(
            dimension_semantics=("parallel", "parallel"), vmem_limit_bytes=VMEM_LIMIT),
        name="hgrn2",
    )(proj, proj, proj, proj, proj, lb.reshape(1, -1), norm_w.reshape(1, -1))


ATT_TQ = 256
LANES = 128


def _group_mean_sq(x, g_ref):
    x2 = x * x
    hi = x2.astype(jnp.bfloat16)
    lo = (x2 - hi.astype(jnp.float32)).astype(jnp.bfloat16)
    g = g_ref[...]
    return (jnp.dot(hi, g, preferred_element_type=jnp.float32)
            + jnp.dot(lo, g, preferred_element_type=jnp.float32))


def _norm_rope(x, g_ref, w, cos, sin_signed):
    n = x.shape[1]
    y = x * lax.rsqrt(_group_mean_sq(x, g_ref) + NORM_EPS) * w
    lane = lax.broadcasted_iota(jnp.int32, y.shape, 1)
    partner = jnp.where(lane % 2 == 0, pltpu.roll(y, n - 1, axis=1), pltpu.roll(y, 1, axis=1))
    return y * cos + partner * sin_signed


def _dup_halves(x):
    lane = lax.broadcasted_iota(jnp.int32, x.shape, 1)
    sw = pltpu.roll(x, ATT_HEAD_DIM, axis=1)
    lo = lane < ATT_HEAD_DIM
    return jnp.where(lo, x, sw), jnp.where(lo, sw, x)


def _gqa_kernel(q_ref, k_ref, v_ref, cq_ref, sq_ref, ck_ref, sk_ref, gq_ref, gk_ref, qw_ref, kw_ref,
                o_ref, kd_ref, vd_ref):
    @pl.when(pl.program_id(1) == 0)
    def _():
        kr = _norm_rope(k_ref[...], gk_ref, kw_ref[...], ck_ref[...], sk_ref[...])
        k0, k1 = _dup_halves(kr)
        kd_ref[0] = k0.astype(jnp.bfloat16)
        kd_ref[1] = k1.astype(jnp.bfloat16)
        v0, v1 = _dup_halves(v_ref[...])
        vd_ref[0] = v0.astype(jnp.bfloat16)
        vd_ref[1] = v1.astype(jnp.bfloat16)

    tq = q_ref.shape[0]
    reps = ATT_INNER // LANES
    cos = jnp.concatenate([cq_ref[...]] * reps, axis=1)
    sin = jnp.concatenate([sq_ref[...]] * reps, axis=1)
    qr = _norm_rope(q_ref[...], gq_ref, qw_ref[...], cos, sin) * (ATT_HEAD_DIM ** -0.5)
    lane = lax.broadcasted_iota(jnp.int32, (tq, LANES), 1)
    lo = lane < ATT_HEAD_DIM
    rep = ATT_Q_HEADS // ATT_KV_HEADS
    for pair in range(ATT_Q_HEADS // 2):
        grp = (2 * pair) // rep
        qp = qr[:, pair * LANES:(pair + 1) * LANES]
        q2 = jnp.concatenate([jnp.where(lo, qp, 0.0), jnp.where(lo, 0.0, qp)], axis=0).astype(jnp.bfloat16)
        s = lax.dot_general(q2, kd_ref[grp], (((1,), (1,)), ((), ())), preferred_element_type=jnp.float32)
        s = s - jnp.max(s, axis=-1, keepdims=True)
        p = jnp.exp(s)
        l = jnp.sum(p, axis=-1, keepdims=True)
        o2 = jnp.dot(p.astype(jnp.bfloat16), vd_ref[grp], preferred_element_type=jnp.float32) / l
        o_ref[:, pair * LANES:(pair + 1) * LANES] = jnp.where(lo, o2[:tq], o2[tq:])


def gqa(proj, cos_t, sin_t, gq, q_w, k_w, bsz, seq):
    m = proj.shape[0]
    tq = ATT_TQ
    nq = seq // tq
    kvw = ATT_KV_HEADS * ATT_HEAD_DIM
    full = lambda shape: pl.BlockSpec(shape, lambda b, i: (0, 0))
    return pl.pallas_call(
        _gqa_kernel,
        grid=(bsz, nq),
        in_specs=[pl.BlockSpec((tq, ATT_INNER), lambda b, i: (b * nq + i, 1536 // ATT_INNER)),
                  pl.BlockSpec((seq, kvw), lambda b, i: (b, 2048 // kvw)),
                  pl.BlockSpec((seq, kvw), lambda b, i: (b, 2176 // kvw)),
                  pl.BlockSpec((tq, LANES), lambda b, i: (i, 0)),
                  pl.BlockSpec((tq, LANES), lambda b, i: (i, 0)),
                  full((seq, LANES)), full((seq, LANES)),
                  full((ATT_INNER, ATT_INNER)), full((LANES, LANES)),
                  full((1, ATT_INNER)), full((1, LANES))],
        out_specs=pl.BlockSpec((tq, ATT_INNER), lambda b, i: (b * nq + i, 0)),
        out_shape=jax.ShapeDtypeStruct((m, ATT_INNER), jnp.float32),
        scratch_shapes=[pltpu.VMEM((ATT_KV_HEADS, seq, LANES), jnp.bfloat16),
                        pltpu.VMEM((ATT_KV_HEADS, seq, LANES), jnp.bfloat16)],
        compiler_params=pltpu.CompilerParams(
            dimension_semantics=("parallel", "arbitrary"), vmem_limit_bytes=VMEM_LIMIT),
        name="gqa",
    )(proj, proj, proj, cos_t, sin_t, cos_t, sin_t, gq, gq[:LANES, :LANES],
      jnp.tile(q_w, ATT_Q_HEADS).reshape(1, -1), jnp.tile(k_w, ATT_KV_HEADS).reshape(1, -1))


def rope_lane_tables(seq):
    cos, sin = axial_rope_tables(seq)
    cos2 = jnp.repeat(cos, 2, axis=1)
    sin2 = jnp.repeat(sin, 2, axis=1) * jnp.tile(jnp.array([-1.0, 1.0], jnp.float32), ATT_HEAD_DIM // 2)
    return jnp.tile(cos2, (1, 2)), jnp.tile(sin2, (1, 2))


def head_mean_matrix():
    idx = jnp.arange(ATT_INNER) // ATT_HEAD_DIM
    return jnp.where(idx[:, None] == idx[None, :], 1.0 / ATT_HEAD_DIM, 0.0).astype(jnp.bfloat16)


NT_DIMS = (((1,), (1,)), ((), ()))
LN_TM = 512


def _layer_norm(x, g, b):
    mu = jnp.mean(x, axis=-1, keepdims=True)
    xc = x - mu
    var = jnp.mean(xc * xc, axis=-1, keepdims=True)
    return xc * lax.rsqrt(var + LN_EPS) * g + b


def _bf16_split(x):
    hi = x.astype(jnp.bfloat16)
    return hi, (x - hi.astype(jnp.float32)).astype(jnp.bfloat16)


def _mix_ln_router_kernel(ya_ref, yb_ref, x_ref, nw_ref, wa_ref, wb_ref, g_ref, b_ref, rwh_ref, rwl_ref,
                          x1_ref, x1b_ref, aff_ref, *, rms_ya):
    ya = ya_ref[...]
    if rms_ya:
        ya = ya * lax.rsqrt(jnp.mean(ya * ya, axis=-1, keepdims=True) + NORM_EPS) * nw_ref[...]
    mix = (jnp.dot(ya.astype(jnp.bfloat16), wa_ref[...], preferred_element_type=jnp.float32)
           + jnp.dot(yb_ref[...].astype(jnp.bfloat16), wb_ref[...], preferred_element_type=jnp.float32))
    x1 = _layer_norm(DEEPNORM_ALPHA * x_ref[...] + mix, g_ref[...], b_ref[...])
    x1_ref[...] = x1
    x1b_ref[...] = x1.astype(jnp.bfloat16)
    hi, lo = _bf16_split(x1)
    rwh = rwh_ref[...]
    lt = (lax.dot_general(rwh, hi, NT_DIMS, preferred_element_type=jnp.float32)
          + lax.dot_general(rwl_ref[...], hi, NT_DIMS, preferred_element_type=jnp.float32)
          + lax.dot_general(rwh, lo, NT_DIMS, preferred_element_type=jnp.float32))
    e = jnp.exp(lt - jnp.max(lt, axis=0, keepdims=True))
    aff_ref[0] = e / jnp.sum(e, axis=0, keepdims=True)


def mix_ln_router(ya, yb, x, w_out, ln_g, ln_b, router_w, bsz, seq, ya_norm_w=None):
    m, d = x.shape
    tm = LN_TM
    nt = seq // tm
    ka = ya.shape[1]
    rms_ya = ya_norm_w is not None
    nw = (ya_norm_w if rms_ya else jnp.ones((ka,), jnp.float32)).reshape(1, ka)
    wa = w_out[:ka].astype(jnp.bfloat16)
    wb = w_out[ka:].astype(jnp.bfloat16)
    rwh, rwl = _bf16_split(router_w.T)
    ne = router_w.shape[1]
    row = lambda w: pl.BlockSpec((tm, w), lambda i: (i, 0))
    full = lambda a: pl.BlockSpec(a.shape, lambda i: (0, 0))
    g2, b2 = ln_g.reshape(1, d), ln_b.reshape(1, d)
    return pl.pallas_call(
        functools.partial(_mix_ln_router_kernel, rms_ya=rms_ya),
        grid=(m // tm,),
        in_specs=[row(ka), row(yb.shape[1]), row(d), full(nw), full(wa), full(wb), full(g2), full(b2),
                  full(rwh), full(rwl)],
        out_specs=[row(d), row(d), pl.BlockSpec((1, ne, tm), lambda i: (i // nt, 0, i % nt))],
        out_shape=[jax.ShapeDtypeStruct((m, d), jnp.float32),
                   jax.ShapeDtypeStruct((m, d), jnp.bfloat16),
                   jax.ShapeDtypeStruct((bsz, ne, seq), jnp.float32)],
        compiler_params=pltpu.CompilerParams(
            dimension_semantics=("parallel",), vmem_limit_bytes=VMEM_LIMIT),
        name="mix_ln_router",
    )(ya, yb, x, nw, wa, wb, g2, b2, rwh, rwl)


ROUTE_BLK = 256


def _excl_cumsum_lanes(mask, tri):
    n = mask.shape[1]
    blk = tri.shape[0]
    run = jnp.zeros((mask.shape[0], 1), jnp.float32)
    outs = []
    for k in range(n // blk):
        mb = mask[:, k * blk:(k + 1) * blk]
        outs.append(jnp.dot(mb.astype(jnp.bfloat16), tri, preferred_element_type=jnp.float32) + run)
        run = run + jnp.sum(mb, axis=-1, keepdims=True)
    return jnp.concatenate(outs, axis=1)


def _route_kernel(aff_ref, tri_ref, pos_ref, gsel_ref, *, cap):
    a = aff_ref[0]
    capf = jnp.float32(cap)
    bits = jnp.zeros((a.shape[0], 1), jnp.int32)
    for bit in range(30, -1, -1):
        cand = bits | (1 << bit)
        cnt = jnp.sum(jnp.where(a >= pltpu.bitcast(cand, jnp.float32), 1.0, 0.0), axis=-1, keepdims=True)
        bits = jnp.where(cnt >= capf, cand, bits)
    thr = pltpu.bitcast(bits, jnp.float32)
    gt = jnp.where(a > thr, 1.0, 0.0)
    eq = jnp.where(a == thr, 1.0, 0.0)
    need = capf - jnp.sum(gt, axis=-1, keepdims=True)
    tri = tri_ref[...]
    sel = gt + eq * jnp.where(_excl_cumsum_lanes(eq, tri) < need, 1.0, 0.0)
    pos = _excl_cumsum_lanes(sel, tri)
    chosen = sel > 0.5
    pos_ref[0] = jnp.where(chosen, pos, -1.0)
    gsel_ref[0] = jnp.where(chosen, a, 0.0)


def route(aff, cap):
    bsz, ne, seq = aff.shape
    blk = min(ROUTE_BLK, seq)
    idx = jnp.arange(blk)
    tri = (idx[:, None] < idx[None, :]).astype(jnp.bfloat16)
    spec = pl.BlockSpec((1, ne, seq), lambda b: (b, 0, 0))
    return pl.pallas_call(
        functools.partial(_route_kernel, cap=cap),
        grid=(bsz,),
        in_specs=[spec, pl.BlockSpec((blk, blk), lambda b: (0, 0))],
        out_specs=[spec, spec],
        out_shape=[jax.ShapeDtypeStruct((bsz, ne, seq), jnp.float32),
                   jax.ShapeDtypeStruct((bsz, ne, seq), jnp.float32)],
        compiler_params=pltpu.CompilerParams(
            dimension_semantics=("parallel",), vmem_limit_bytes=VMEM_LIMIT),
        name="route",
    )(aff, tri)


GATHER_NE = 4


def _gather_kernel(pos_ref, gsel_ref, x_ref, o_ref, gs_ref):
    nge, cap, d = o_ref.shape
    seq = x_ref.shape[0]
    slot = lax.broadcasted_iota(jnp.int32, (cap, seq), 0).astype(jnp.float32)
    hots = []
    for e in range(nge):
        hit = pos_ref[0, e] == slot
        hots.append(jnp.where(hit, 1.0, 0.0).astype(jnp.bfloat16))
        gs_ref[e] = jnp.sum(jnp.where(hit, gsel_ref[0, e], 0.0), axis=-1, keepdims=True)
    rows = jnp.dot(jnp.concatenate(hots, axis=0), x_ref[...], preferred_element_type=jnp.float32)
    o_ref[...] = rows.astype(jnp.bfloat16).reshape(nge, cap, d)


def gather_tokens(pos, gsel, x1b, cap):
    bsz, ne, seq = pos.shape
    d = x1b.shape[1]
    nge = GATHER_NE
    sel = pl.BlockSpec((1, nge, 1, seq), lambda b, g: (b, g, 0, 0))
    return pl.pallas_call(
        _gather_kernel,
        grid=(bsz, ne // nge),
        in_specs=[sel, sel, pl.BlockSpec((seq, d), lambda b, g: (b, 0))],
        out_specs=[pl.BlockSpec((nge, cap, d), lambda b, g: (g, b, 0)),
                   pl.BlockSpec((nge, cap, 1), lambda b, g: (g, b, 0))],
        out_shape=[jax.ShapeDtypeStruct((ne, bsz * cap, d), jnp.bfloat16),
                   jax.ShapeDtypeStruct((ne, bsz * cap, 1), jnp.float32)],
        compiler_params=pltpu.CompilerParams(
            dimension_semantics=("parallel", "parallel"), vmem_limit_bytes=VMEM_LIMIT),
        name="gather_tokens",
    )(pos.reshape(bsz, ne, 1, seq), gsel.reshape(bsz, ne, 1, seq), x1b)


COMB_TR = 512


def _combine_ln_kernel(ys_ref, post_ref, x1_ref, g_ref, b_ref, o_ref):
    post = post_ref[0]
    ne, cap, d = ys_ref.shape
    lane = lax.broadcasted_iota(jnp.int32, (post.shape[0], cap), 1).astype(jnp.float32)
    hots = [jnp.where(post[:, e:e + 1] == lane, 1.0, 0.0).astype(jnp.bfloat16) for e in range(ne)]
    ffn = jnp.dot(jnp.concatenate(hots, axis=1), ys_ref[...].reshape(ne * cap, d),
                  preferred_element_type=jnp.float32)
    o_ref[...] = _layer_norm(DEEPNORM_ALPHA * x1_ref[...] + ffn, g_ref[...], b_ref[...])


def combine_ln(ys, post, x1, ln_g, ln_b, cap):
    bsz, seq, ne = post.shape
    m, d = x1.shape
    tr = min(COMB_TR, seq)
    nr = seq // tr
    g2, b2 = ln_g.reshape(1, d), ln_b.reshape(1, d)
    vec = pl.BlockSpec((1, d), lambda b, r: (0, 0))
    rows = pl.BlockSpec((tr, d), lambda b, r: (b * nr + r, 0))
    return pl.pallas_call(
        _combine_ln_kernel,
        grid=(bsz, nr),
        in_specs=[pl.BlockSpec((ne, cap, d), lambda b, r: (0, b, 0)),
                  pl.BlockSpec((1, tr, ne), lambda b, r: (b, r, 0)), rows, vec, vec],
        out_specs=rows,
        out_shape=jax.ShapeDtypeStruct((m, d), jnp.float32),
        compiler_params=pltpu.CompilerParams(
            dimension_semantics=("parallel", "parallel"), vmem_limit_bytes=VMEM_LIMIT),
        name="combine_ln",
    )(ys, post, x1, g2, b2)


def moe_block(ya, yb, x2d, w_out, ln1_g, ln1_b, router_w, w1, w3, w2, layer, ln2_g, ln2_b, bsz, seq,
              ya_norm_w=None):
    cap = CAPACITY_FACTOR * seq // N_EXPERTS
    x1, x1b, aff = mix_ln_router(ya, yb, x2d, w_out, ln1_g, ln1_b, router_w, bsz, seq, ya_norm_w)
    pos, gsel = route(aff, cap)
    xs, gs = gather_tokens(pos, gsel, x1b, cap)
    ys = expert_ffn(xs, gs, w1, w3, w2, layer)
    return combine_ln(ys, pos.transpose(0, 2, 1), x1, ln2_g, ln2_b, cap)


SSD_GH = SSD_HEADS // SSD_GROUPS
SSD_GW = SSD_GH * SSD_HEAD_DIM


def _shift_rows(x, s, row):
    n = x.shape[0]
    if s == 0:
        return x
    y = pltpu.roll(x, (-s) % n, axis=0)
    return jnp.where(row >= -s, y, 0.0) if s < 0 else jnp.where(row < n - s, y, 0.0)


def _conv_silu(x, w_ref, b_ref):
    row = lax.broadcasted_iota(jnp.int32, x.shape, 0)
    pad = (SSD_CONV_K - 1) // 2
    y = b_ref[...] + jnp.zeros_like(x)
    for k in range(SSD_CONV_K):
        y = y + w_ref[k:k + 1, :] * _shift_rows(x, k - pad, row)
    return y * jax.nn.sigmoid(y)


def _cumsum_rows(x, row, reverse):
    n = x.shape[0]
    s = 1
    while s < n:
        if reverse:
            x = x + jnp.where(row + s < n, pltpu.roll(x, n - s, axis=0), 0.0)
        else:
            x = x + jnp.where(row >= s, pltpu.roll(x, s, axis=0), 0.0)
        s *= 2
    return x


def _select_lanes(v, sel):
    hi = v.astype(jnp.bfloat16)
    r1 = v - hi.astype(jnp.float32)
    mid = r1.astype(jnp.bfloat16)
    lo = (r1 - mid.astype(jnp.float32)).astype(jnp.bfloat16)
    pieces = jnp.concatenate([hi, mid, lo], axis=1)
    return jnp.dot(pieces, jnp.concatenate([sel, sel, sel], axis=0), preferred_element_type=jnp.float32)


def _ssd_chunk(ux, ub, uc, dtv, la, st_ref, sele, selc_ref, reverse):
    q = SSD_CHUNK
    bf = jnp.bfloat16
    cols = [SSD_GH + h if reverse else h for h in range(SSD_GH)]
    row = lax.broadcasted_iota(jnp.int32, (q, LANES), 0)
    lane = lax.broadcasted_iota(jnp.int32, (q, LANES), 1)
    lane_lo = lane < SSD_HEAD_DIM
    tri = (row <= lane) if reverse else (row >= lane)
    cum = _cumsum_rows(la, row, reverse)
    cum_t = cum.T
    edge = 0 if reverse else q - 1
    cum_e = _select_lanes(cum, sele)
    tot_e = cum_e[edge:edge + 1, :]
    ub16 = ub.astype(bf)
    uc16 = uc.astype(bf)
    cb = lax.dot_general(uc16, ub16, NT_DIMS, preferred_element_type=jnp.float32)
    xdt = ux * _select_lanes(dtv, sele)
    ys = []
    for p in range(SSD_GH // 2):
        xp = xdt[:, p * LANES:(p + 1) * LANES]
        acc = None
        for half in range(2):
            c = cols[2 * p + half]
            seg = cum[:, c:c + 1] - cum_t[c:c + 1, :]
            lm = jnp.where(tri, jnp.exp(seg), 0.0)
            xh = jnp.where(lane_lo, xp, 0.0) if half == 0 else jnp.where(lane_lo, 0.0, xp)
            t = jnp.dot((cb * lm).astype(bf), xh.astype(bf), preferred_element_type=jnp.float32)
            acc = t if acc is None else acc + t
        ys.append(acc)
    y = jnp.concatenate(ys, axis=1)
    st = st_ref[...]
    y = y + jnp.exp(cum_e) * jnp.dot(uc16, st.astype(bf), preferred_element_type=jnp.float32)
    xe = (xdt * jnp.exp(tot_e - cum_e)).astype(bf)
    new = lax.dot_general(ub16, xe, (((0,), (0,)), ((), ())), preferred_element_type=jnp.float32)
    st_ref[...] = st * jnp.exp(tot_e) + new
    return y


def _ssd_kernel(z_ref, x_ref, b_ref, c_ref, dt_ref, cwx_ref, cwb_ref, cwc_ref, cbx_ref, cbb_ref, cbc_ref,
                dtb_ref, aneg_ref, dsk_ref, sele_ref, selc_ref, o_ref, ux_ref, ub_ref, uc_ref, dtv_ref, la_ref,
                yf_ref, yb_ref, stf_ref, stb_ref):
    ux_ref[...] = _conv_silu(x_ref[...], cwx_ref, cbx_ref)
    ub_ref[...] = _conv_silu(b_ref[...], cwb_ref, cbb_ref)
    uc_ref[...] = _conv_silu(c_ref[...], cwc_ref, cbc_ref)
    t = dt_ref[...] + dtb_ref[...]
    dtv = jnp.maximum(t, 0.0) + jnp.log1p(jnp.exp(-jnp.abs(t)))
    dtv_ref[...] = dtv
    la_ref[...] = dtv * aneg_ref[...]
    stf_ref[...] = jnp.zeros_like(stf_ref)
    stb_ref[...] = jnp.zeros_like(stb_ref)
    q = SSD_CHUNK
    nchunk = x_ref.shape[0] // q

    def body(c, carry):
        rf = pl.ds(pl.multiple_of(c * q, q), q)
        rb = pl.ds(pl.multiple_of((nchunk - 1 - c) * q, q), q)
        yf_ref[rf, :] = _ssd_chunk(ux_ref[rf, :], ub_ref[rf, :], uc_ref[rf, :], dtv_ref[rf, :], la_ref[rf, :],
                                   stf_ref, sele_ref[0], selc_ref, False)
        yb_ref[rb, :] = _ssd_chunk(ux_ref[rb, :], ub_ref[rb, :], uc_ref[rb, :], dtv_ref[rb, :], la_ref[rb, :],
                                   stb_ref, sele_ref[1], selc_ref, True)
        return carry

    lax.fori_loop(0, nchunk, body, 0)
    z = z_ref[...]
    y = yf_ref[...] + yb_ref[...] + ux_ref[...] * dsk_ref[...]
    o_ref[...] = y * (z * jax.nn.sigmoid(z))


def ssd(proj, dt, conv_w, conv_b, dt_bias_g, a_neg_g, d_skip_g, bsz, seq):
    m = proj.shape[0]
    gw, n = SSD_GW, SSD_STATE
    f32 = jnp.float32
    kk = SSD_CONV_K
    cb2 = conv_b.reshape(1, -1)
    in_specs = [
        pl.BlockSpec((seq, gw), lambda b, g: (b, g)),
        pl.BlockSpec((seq, gw), lambda b, g: (b, SSD_INNER // gw + g)),
        pl.BlockSpec((seq, n), lambda b, g: (b, 2 * SSD_INNER // n + g)),
        pl.BlockSpec((seq, n), lambda b, g: (b, 2 * SSD_INNER // n + SSD_GROUPS + g)),
        pl.BlockSpec((seq, LANES), lambda b, g: (b, g)),
        pl.BlockSpec((kk, gw), lambda b, g: (0, g)),
        pl.BlockSpec((kk, n), lambda b, g: (0, SSD_INNER // n + g)),
        pl.BlockSpec((kk, n), lambda b, g: (0, SSD_INNER // n + SSD_GROUPS + g)),
        pl.BlockSpec((1, gw), lambda b, g: (0, g)),
        pl.BlockSpec((1, n), lambda b, g: (0, SSD_INNER // n + g)),
        pl.BlockSpec((1, n), lambda b, g: (0, SSD_INNER // n + SSD_GROUPS + g)),
        pl.BlockSpec((1, LANES), lambda b, g: (0, g)),
        pl.BlockSpec((1, LANES), lambda b, g: (0, g)),
        pl.BlockSpec((1, gw), lambda b, g: (0, g)),
        pl.BlockSpec((2, LANES, gw), lambda b, g: (0, 0, 0)),
        pl.BlockSpec((2 * SSD_GH, LANES, LANES), lambda b, g: (0, 0, 0)),
    ]
    src = jnp.arange(LANES)
    sele = jnp.stack([(src[:, None] == d * SSD_GH + jnp.arange(gw)[None, :] // SSD_HEAD_DIM) for d in range(2)])
    selc = jnp.stack([jnp.broadcast_to(src[:, None] == c, (LANES, LANES)) for c in range(2 * SSD_GH)])
    return pl.pallas_call(
        _ssd_kernel,
        grid=(bsz, SSD_GROUPS),
        in_specs=in_specs,
        out_specs=pl.BlockSpec((seq, gw), lambda b, g: (b, g)),
        out_shape=jax.ShapeDtypeStruct((m, SSD_INNER), f32),
        scratch_shapes=[pltpu.VMEM((seq, gw), f32), pltpu.VMEM((seq, n), f32), pltpu.VMEM((seq, n), f32),
                        pltpu.VMEM((seq, LANES), f32), pltpu.VMEM((seq, LANES), f32),
                        pltpu.VMEM((seq, gw), f32), pltpu.VMEM((seq, gw), f32),
                        pltpu.VMEM((n, gw), f32), pltpu.VMEM((n, gw), f32)],
        compiler_params=pltpu.CompilerParams(
            dimension_semantics=("parallel", "parallel"), vmem_limit_bytes=VMEM_LIMIT),
        name="ssd",
    )(proj, proj, proj, proj, dt, conv_w, conv_w, conv_w, cb2, cb2, cb2, dt_bias_g, a_neg_g, d_skip_g,
      sele.astype(jnp.bfloat16), selc.astype(jnp.bfloat16))


def _group_lanes(v):
    out = jnp.zeros((SSD_GROUPS, LANES), v.dtype)
    for g in range(SSD_GROUPS):
        out = out.at[g, :SSD_GH].set(v[0, g * SSD_GH:(g + 1) * SSD_GH])
        out = out.at[g, SSD_GH:2 * SSD_GH].set(v[1, g * SSD_GH:(g + 1) * SSD_GH])
    return out.reshape(1, -1)


def _dt_weight(w_dt):
    d = w_dt.shape[0]
    out = jnp.zeros((d, SSD_GROUPS, LANES), w_dt.dtype)
    for g in range(SSD_GROUPS):
        out = out.at[:, g, :SSD_GH].set(w_dt[:, g * SSD_GH:(g + 1) * SSD_GH])
        out = out.at[:, g, SSD_GH:2 * SSD_GH].set(w_dt[:, SSD_HEADS + g * SSD_GH:SSD_HEADS + (g + 1) * SSD_GH])
    return out.reshape(d, -1)


def _dt_proj_kernel(x_ref, wh_ref, wl_ref, o_ref):
    hi, lo = _bf16_split(x_ref[...])
    wh = wh_ref[...]
    o_ref[...] = (jnp.dot(hi, wh, preferred_element_type=jnp.float32)
                  + jnp.dot(hi, wl_ref[...], preferred_element_type=jnp.float32)
                  + jnp.dot(lo, wh, preferred_element_type=jnp.float32))


def dt_proj(x, w_dt_g):
    m, k = x.shape
    n = w_dt_g.shape[1]
    wh, wl = _bf16_split(w_dt_g)
    tm = LN_TM
    return pl.pallas_call(
        _dt_proj_kernel,
        grid=(m // tm,),
        in_specs=[pl.BlockSpec((tm, k), lambda i: (i, 0)),
                  pl.BlockSpec((k, n), lambda i: (0, 0)), pl.BlockSpec((k, n), lambda i: (0, 0))],
        out_specs=pl.BlockSpec((tm, n), lambda i: (i, 0)),
        out_shape=jax.ShapeDtypeStruct((m, n), jnp.float32),
        compiler_params=pltpu.CompilerParams(
            dimension_semantics=("parallel",), vmem_limit_bytes=VMEM_LIMIT),
        name="dt_proj",
    )(x, wh, wl)


def _pool_kernel(u_ref, w_ref, sc_ref, o_ref):
    n = u_ref.shape[0]
    row = lax.broadcasted_iota(jnp.int32, (n, POOL_GROUP), 0)
    for gi, win in enumerate(POOL_WINDOWS):
        u = u_ref[:, gi * POOL_GROUP:(gi + 1) * POOL_GROUP]
        half = win // 2
        fwd, bwd = u, u
        s = 1
        while s < half:
            fwd = fwd + _shift_rows(fwd, s, row)
            bwd = bwd + _shift_rows(bwd, -s, row)
            s *= 2
        total = fwd + _shift_rows(bwd, -1, row)
        cnt = (jnp.minimum(row + half, n) - jnp.maximum(row - half, 0)).astype(jnp.float32)
        pooled = total / cnt - u
        mixed = jnp.dot(pooled.astype(jnp.bfloat16), w_ref[gi], preferred_element_type=jnp.float32)
        o_ref[:, gi * POOL_GROUP:(gi + 1) * POOL_GROUP] = mixed * sc_ref[:, gi * POOL_GROUP:(gi + 1) * POOL_GROUP]


def multiscale_pool_kernel(proj, pool_w, pool_scale, bsz, seq):
    m = proj.shape[0]
    return pl.pallas_call(
        _pool_kernel,
        grid=(bsz,),
        in_specs=[pl.BlockSpec((seq, POOL_INNER), lambda b: (b, 0)),
                  pl.BlockSpec(pool_w.shape, lambda b: (0, 0, 0)),
                  pl.BlockSpec((1, POOL_INNER), lambda b: (0, 0))],
        out_specs=pl.BlockSpec((seq, POOL_INNER), lambda b: (b, 0)),
        out_shape=jax.ShapeDtypeStruct((m, POOL_INNER), jnp.float32),
        compiler_params=pltpu.CompilerParams(
            dimension_semantics=("parallel",), vmem_limit_bytes=VMEM_LIMIT),
        name="multiscale_pool",
    )(proj, pool_w.astype(jnp.bfloat16), pool_scale.reshape(1, -1))


def axial_rope_tables(seq):
    rows = seq // GRID_W
    row = jnp.repeat(jnp.arange(rows), GRID_W).astype(jnp.float32)
    col = jnp.tile(jnp.arange(GRID_W), rows).astype(jnp.float32)
    axis_dims = ATT_HEAD_DIM // 2
    freqs = ROPE_THETA ** (-jnp.arange(0, axis_dims, 2, dtype=jnp.float32) / axis_dims)
    ang = jnp.concatenate([row[:, None] * freqs, col[:, None] * freqs], axis=-1)
    return jnp.cos(ang), jnp.sin(ang)


def kernel(x, w_in_ab, ssm_conv_w, ssm_conv_b, ssm_dt_bias, ssm_a_log, ssm_d, ssm_norm_w, attn_q_norm, attn_k_norm, w_out_ab, w_in_cd, pool_w, pool_scale, hgrn_lb_logits, hgrn_norm_w, w_out_cd, router_w, moe_w1, moe_w3, moe_w2, ln1_g, ln1_b, ln2_g, ln2_b):
    bsz, seq, d = x.shape
    m = bsz * seq
    cos_t, sin_t = rope_lane_tables(seq)
    gq = head_mean_matrix()
    lb_all = jnp.cumsum(jax.nn.softmax(hgrn_lb_logits, axis=0), axis=0)
    lb_all = lb_all - lb_all[0]
    x2 = x.reshape(m, d)
    for layer in range(DEPTH):
        j = layer // 2
        if layer % 2 == 0:
            w = w_in_ab[j]
            w_main = jnp.concatenate([w[:, :1536], w[:, 1552:]], axis=1).astype(jnp.bfloat16)
            proj = matmul(x2, w_main)
            dt = dt_proj(x2, _dt_weight(w[:, 1536:1552]))
            ya = ssd(proj, dt, ssm_conv_w[j], ssm_conv_b[j], _group_lanes(ssm_dt_bias[j]),
                     _group_lanes(-jnp.exp(ssm_a_log[j])), jnp.repeat(ssm_d[j], SSD_HEAD_DIM).reshape(1, -1),
                     bsz, seq)
            yb = gqa(proj, cos_t, sin_t, gq, attn_q_norm[j], attn_k_norm[j], bsz, seq)
            w_out, ya_norm_w = w_out_ab[j], ssm_norm_w[j]
        else:
            proj = matmul(x2, w_in_cd[j].astype(jnp.bfloat16))
            ya = multiscale_pool_kernel(proj, pool_w[j], pool_scale[j], bsz, seq)
            yb = hgrn2(proj, lb_all[layer], hgrn_norm_w[j], bsz, seq)
            w_out, ya_norm_w = w_out_cd[j], None
        x2 = moe_block(ya, yb, x2, w_out, ln1_g[layer], ln1_b[layer], router_w[layer],
                       moe_w1, moe_w3, moe_w2, layer, ln2_g[layer], ln2_b[layer], bsz, seq, ya_norm_w)
    return x2.reshape(bsz, seq, d)
```

```python
import functools
import math

import jax
import jax.numpy as jnp
from jax import lax
from jax.experimental import pallas as pl
from jax.experimental.pallas import tpu as pltpu

DEPTH = 4
GRID_W = 64
NORM_EPS = 1e-6
LN_EPS = 1e-5

SSD_HEADS = 8
SSD_HEAD_DIM = 64
SSD_INNER = 512
SSD_GROUPS = 2
SSD_STATE = 128
SSD_CONV_K = 5
SSD_CHUNK = 128

ATT_Q_HEADS = 8
ATT_KV_HEADS = 2
ATT_HEAD_DIM = 64
ATT_INNER = 512
ROPE_THETA = 10000.0

POOL_WINDOWS = (2, 4, 8, 16)
POOL_GROUP = 128
POOL_INNER = 512

HGRN_HEADS = 4
HGRN_HEAD_DIM = 128
HGRN_INNER = 512

N_EXPERTS = 16
CAPACITY_FACTOR = 2

DEEPNORM_ALPHA = (2 * DEPTH) ** 0.25

LANES = 128
VMEM_LIMIT = 56 * 1024 * 1024
NT_DIMS = (((1,), (1,)), ((), ()))


def _mm_kernel(x_ref, w_ref, o_ref):
    o_ref[...] = jnp.dot(x_ref[...].astype(jnp.bfloat16), w_ref[...],
                         preferred_element_type=jnp.float32)


MM_TM = 2048
MM_TN = 768


def matmul(x, w_bf16, *, tm=MM_TM, tn=MM_TN):
    m, k = x.shape
    n = w_bf16.shape[1]
    return pl.pallas_call(
        _mm_kernel,
        grid=(m // tm, n // tn),
        in_specs=[pl.BlockSpec((tm, k), lambda i, j: (i, 0)),
                  pl.BlockSpec((k, tn), lambda i, j: (0, j))],
        out_specs=pl.BlockSpec((tm, tn), lambda i, j: (i, j)),
        out_shape=jax.ShapeDtypeStruct((m, n), jnp.float32),
        compiler_params=pltpu.CompilerParams(
            dimension_semantics=("parallel", "parallel"), vmem_limit_bytes=VMEM_LIMIT),
        name="matmul",
    )(x, w_bf16)


FFN_TF = 512


FFN_ROWS = 512


def _ffn_kernel(xs_ref, gs_ref, w1_ref, w3_ref, w2_ref, o_ref, acc_ref):
    f = pl.program_id(1)

    @pl.when(f == 0)
    def _():
        acc_ref[...] = jnp.zeros_like(acc_ref)

    w1 = w1_ref[...].astype(jnp.bfloat16)
    w3 = w3_ref[...].astype(jnp.bfloat16)
    w2 = w2_ref[...].astype(jnp.bfloat16)
    rc = min(FFN_ROWS, xs_ref.shape[0])
    for c in range(xs_ref.shape[0] // rc):
        rows = slice(c * rc, (c + 1) * rc)
        xc = xs_ref[rows, :]
        h1 = jnp.dot(xc, w1, preferred_element_type=jnp.float32)
        h3 = jnp.dot(xc, w3, preferred_element_type=jnp.float32)
        hdn = (h1 * jax.nn.sigmoid(h1) * h3).astype(jnp.bfloat16)
        acc_ref[rows, :] += jnp.dot(hdn, w2, preferred_element_type=jnp.float32)

    @pl.when(f == pl.num_programs(1) - 1)
    def _():
        o_ref[...] = (acc_ref[...] * gs_ref[...]).astype(o_ref.dtype)


def expert_ffn(xs, gs, w1, w3, w2, layer):
    e, r, d = xs.shape
    ff = w1.shape[3]
    tf = FFN_TF
    return pl.pallas_call(
        _ffn_kernel,
        grid=(e, ff // tf),
        in_specs=[pl.BlockSpec((None, r, d), lambda i, j: (i, 0, 0)),
                  pl.BlockSpec((None, r, 1), lambda i, j: (i, 0, 0)),
                  pl.BlockSpec((None, None, d, tf), lambda i, j: (layer, i, 0, j)),
                  pl.BlockSpec((None, None, d, tf), lambda i, j: (layer, i, 0, j)),
                  pl.BlockSpec((None, None, tf, d), lambda i, j: (layer, i, j, 0))],
        out_specs=pl.BlockSpec((None, r, d), lambda i, j: (i, 0, 0)),
        out_shape=jax.ShapeDtypeStruct((e, r, d), jnp.bfloat16),
        scratch_shapes=[pltpu.VMEM((r, d), jnp.float32)],
        compiler_params=pltpu.CompilerParams(
            dimension_semantics=("parallel", "arbitrary"), vmem_limit_bytes=VMEM_LIMIT),
        name="expert_ffn",
    )(xs, gs, w1, w3, w2)


HG_T = 64
HG_UNROLL = 4


def _hgrn_gate(raw, log_lb, log1m_lb, one_m_lb):
    e = jnp.exp(-jnp.abs(raw))
    den = 1.0 + e
    ls = jnp.minimum(raw, 0.0) - jnp.log(den)
    bb = log1m_lb + ls
    mx = jnp.maximum(log_lb, bb)
    logf = mx + jnp.log(1.0 + jnp.exp(-jnp.abs(log_lb - bb)))
    sig_neg = jnp.where(raw >= 0.0, e, 1.0) / den
    return logf * math.log2(math.e), one_m_lb * sig_neg


HG_S = 8


def _hgrn_block(q, k, lf, v, st_ref, backward):
    t, s8 = HG_T, HG_S
    hd = HGRN_HEAD_DIM
    bf = jnp.bfloat16
    row8 = lax.broadcasted_iota(jnp.int32, (s8, hd), 0)
    r8 = lax.broadcasted_iota(jnp.int32, (s8, t), 0)
    lt = lax.broadcasted_iota(jnp.int32, (s8, t), 1)
    sdiag, cloc = [], []
    for i in range(t // s8):
        rows = slice(i * s8, (i + 1) * s8)
        qh, kh_, lfh = q[rows], k[rows], lf[rows]
        w = jnp.zeros_like(qh)
        sc = jnp.zeros((s8, t), jnp.float32)
        for d in range(s8):
            sh = (s8 - d) % s8 if backward else d
            kd = kh_ if d == 0 else pltpu.roll(kh_, sh, axis=0)
            lfd = lfh if d == 0 else pltpu.roll(lfh, sh, axis=0)
            p = qh * kd * jnp.exp2(w)
            valid = (r8 + d <= s8 - 1) if backward else (r8 >= d)
            col = jnp.where(valid, (r8 + d if backward else r8 - d) + i * s8, -1)
            sc = jnp.where(lt == col, jnp.sum(p, axis=-1, keepdims=True), sc)
            w = w + lfd
        sdiag.append(sc)
        c = lfh
        s = 1
        while s < s8:
            if backward:
                c = c + jnp.where(row8 + s <= s8 - 1, pltpu.roll(c, s8 - s, axis=0), 0.0)
            else:
                c = c + jnp.where(row8 >= s, pltpu.roll(c, s, axis=0), 0.0)
            s *= 2
        cloc.append(c)
    scores = jnp.concatenate(sdiag, axis=0)
    cl = jnp.concatenate(cloc, axis=0)
    row = lax.broadcasted_iota(jnp.int32, (t, hd), 0)
    ti = lax.broadcasted_iota(jnp.int32, (t, t), 0)
    tj = lax.broadcasted_iota(jnp.int32, (t, t), 1)
    h = s8
    while h < t:
        is_far = ((row // h) % 2 == 0) if backward else ((row // h) % 2 == 1)
        parts = []
        for blk in range(t // (2 * h)):
            e = blk * 2 * h + (h if backward else h - 1)
            parts.append(jnp.broadcast_to(cl[e:e + 1, :], (2 * h, hd)))
        tn = jnp.concatenate(parts, axis=0) if len(parts) > 1 else parts[0]
        qt = jnp.where(is_far, q * jnp.exp2(cl), 0.0).astype(bf)
        kt = jnp.where(is_far, 0.0, k * jnp.exp2(jnp.where(is_far, 0.0, tn - cl))).astype(bf)
        s_h = lax.dot_general(qt, kt, NT_DIMS, preferred_element_type=jnp.float32)
        if 2 * h < t:
            s_h = jnp.where(ti // (2 * h) == tj // (2 * h), s_h, 0.0)
        scores = scores + s_h
        cl = cl + jnp.where(is_far, tn, 0.0)
        h *= 2
    edge = 0 if backward else t - 1
    tot = cl[edge:edge + 1, :]
    st = st_ref[...]
    vb = v.astype(bf)
    o = jnp.dot(scores.astype(bf), vb, preferred_element_type=jnp.float32)
    o = o + lax.dot_general((q * jnp.exp2(cl)).astype(bf), st.astype(bf), NT_DIMS,
                            preferred_element_type=jnp.float32)
    kh = (k * jnp.exp2(tot - cl)).astype(bf)
    ut = lax.dot_general(vb, kh, (((0,), (0,)), ((), ())), preferred_element_type=jnp.float32)
    st_ref[...] = st * jnp.exp2(tot) + ut
    return o


def _hgrn_kernel(q_ref, ff_ref, fb_ref, i_ref, g_ref, lb_ref, nw_ref, o_ref,
                 lff_ref, kf_ref, lfb_ref, kb_ref, of_ref, ob_ref, stf_ref, stb_ref):
    lb = lb_ref[...]
    log_lb = jnp.log(lb)
    log1m_lb = jnp.log1p(-lb)
    one_m_lb = 1.0 - lb
    lf, kk = _hgrn_gate(ff_ref[...], log_lb, log1m_lb, one_m_lb)
    lff_ref[...] = lf
    kf_ref[...] = kk
    lf, kk = _hgrn_gate(fb_ref[...], log_lb, log1m_lb, one_m_lb)
    lfb_ref[...] = lf
    kb_ref[...] = kk
    stf_ref[...] = jnp.zeros_like(stf_ref)
    stb_ref[...] = jnp.zeros_like(stb_ref)
    nblk = q_ref.shape[0] // HG_T

    def body(n, carry):
        for u in range(HG_UNROLL):
            blk = n * HG_UNROLL + u
            rf = pl.ds(pl.multiple_of(blk * HG_T, HG_T), HG_T)
            rb = pl.ds(pl.multiple_of((nblk - 1 - blk) * HG_T, HG_T), HG_T)
            of_ref[rf, :] = _hgrn_block(q_ref[rf, :], kf_ref[rf, :], lff_ref[rf, :], i_ref[rf, :], stf_ref, False)
            ob_ref[rb, :] = _hgrn_block(q_ref[rb, :], kb_ref[rb, :], lfb_ref[rb, :], i_ref[rb, :], stb_ref, True)
        return carry

    lax.fori_loop(0, nblk // HG_UNROLL, body, 0)
    o = of_ref[...] + ob_ref[...]
    o = o * lax.rsqrt(jnp.mean(o * o, axis=-1, keepdims=True) + NORM_EPS) * nw_ref[...]
    o_ref[...] = o * jax.nn.sigmoid(g_ref[...])


def hgrn2(proj, lb, norm_w, bsz, seq):
    m = proj.shape[0]
    hd = HGRN_HEAD_DIM
    nh = HGRN_HEADS

    def col(base):
        return pl.BlockSpec((seq, hd), lambda b, h: (b, base * nh + h))

    vec = pl.BlockSpec((1, hd), lambda b, h: (0, h))
    f32 = jnp.float32
    return pl.pallas_call(
        _hgrn_kernel,
        grid=(bsz, nh),
        in_specs=[col(1), col(2), col(3), col(4), col(5), vec, vec],
        out_specs=pl.BlockSpec((seq, hd), lambda b, h: (b, h)),
        out_shape=jax.ShapeDtypeStruct((m, HGRN_INNER), f32),
        scratch_shapes=[pltpu.VMEM((seq, hd), f32) for _ in range(6)]
        + [pltpu.VMEM((hd, hd), f32) for _ in range(2)],
        compiler_params=pltpu.CompilerParams(
            dimension_semantics=("parallel", "parallel"), vmem_limit_bytes=VMEM_LIMIT),
        name="hgrn2",
    )(proj, proj, proj, proj, proj, lb.reshape(1, -1), norm_w.reshape(1, -1))


ATT_TQ = 512


def _group_mean_sq(x, g_ref):
    x2 = x * x
    hi = x2.astype(jnp.bfloat16)
    lo = (x2 - hi.astype(jnp.float32)).astype(jnp.bfloat16)
    g = g_ref[...]
    return (jnp.dot(hi, g, preferred_element_type=jnp.float32)
            + jnp.dot(lo, g, preferred_element_type=jnp.float32))


def _norm_rope(x, g_ref, w, cos, sin_signed):
    n = x.shape[1]
    y = x * lax.rsqrt(_group_mean_sq(x, g_ref) + NORM_EPS) * w
    lane = lax.broadcasted_iota(jnp.int32, y.shape, 1)
    partner = jnp.where(lane % 2 == 0, pltpu.roll(y, n - 1, axis=1), pltpu.roll(y, 1, axis=1))
    return y * cos + partner * sin_signed


def _dup_halves(x):
    lane = lax.broadcasted_iota(jnp.int32, x.shape, 1)
    sw = pltpu.roll(x, ATT_HEAD_DIM, axis=1)
    lo = lane < ATT_HEAD_DIM
    return jnp.where(lo, x, sw), jnp.where(lo, sw, x)


def _gqa_kernel(q_ref, k_ref, v_ref, cq_ref, sq_ref, ck_ref, sk_ref, gq_ref, gk_ref, qw_ref, kw_ref,
                o_ref, kd_ref, vd_ref):
    @pl.when(pl.program_id(1) == 0)
    def _():
        kr = _norm_rope(k_ref[...], gk_ref, kw_ref[...], ck_ref[...], sk_ref[...])
        k0, k1 = _dup_halves(kr)
        kd_ref[0] = k0.astype(jnp.bfloat16)
        kd_ref[1] = k1.astype(jnp.bfloat16)
        v0, v1 = _dup_halves(v_ref[...])
        vd_ref[0] = v0.astype(jnp.bfloat16)
        vd_ref[1] = v1.astype(jnp.bfloat16)

    tq = q_ref.shape[0]
    reps = ATT_INNER // LANES
    cos = jnp.concatenate([cq_ref[...]] * reps, axis=1)
    sin = jnp.concatenate([sq_ref[...]] * reps, axis=1)
    qr = _norm_rope(q_ref[...], gq_ref, qw_ref[...], cos, sin) * (ATT_HEAD_DIM ** -0.5)
    lane = lax.broadcasted_iota(jnp.int32, (tq, LANES), 1)
    lo = lane < ATT_HEAD_DIM
    rep = ATT_Q_HEADS // ATT_KV_HEADS
    for pair in range(ATT_Q_HEADS // 2):
        grp = (2 * pair) // rep
        qp = qr[:, pair * LANES:(pair + 1) * LANES]
        q2 = jnp.concatenate([jnp.where(lo, qp, 0.0), jnp.where(lo, 0.0, qp)], axis=0).astype(jnp.bfloat16)
        s = lax.dot_general(q2, kd_ref[grp], (((1,), (1,)), ((), ())), preferred_element_type=jnp.float32)
        s = s - jnp.max(s, axis=-1, keepdims=True)
        p = jnp.exp(s)
        l = jnp.sum(p, axis=-1, keepdims=True)
        o2 = jnp.dot(p.astype(jnp.bfloat16), vd_ref[grp], preferred_element_type=jnp.float32) / l
        o_ref[:, pair * LANES:(pair + 1) * LANES] = jnp.where(lo, o2[:tq], o2[tq:])


def gqa(proj, cos_t, sin_t, gq, q_w, k_w, bsz, seq):
    m = proj.shape[0]
    tq = ATT_TQ
    nq = seq // tq
    kvw = ATT_KV_HEADS * ATT_HEAD_DIM
    full = lambda shape: pl.BlockSpec(shape, lambda b, i: (0, 0))
    return pl.pallas_call(
        _gqa_kernel,
        grid=(bsz, nq),
        in_specs=[pl.BlockSpec((tq, ATT_INNER), lambda b, i: (b * nq + i, 1536 // ATT_INNER)),
                  pl.BlockSpec((seq, kvw), lambda b, i: (b, 2048 // kvw)),
                  pl.BlockSpec((seq, kvw), lambda b, i: (b, 2176 // kvw)),
                  pl.BlockSpec((tq, LANES), lambda b, i: (i, 0)),
                  pl.BlockSpec((tq, LANES), lambda b, i: (i, 0)),
                  full((seq, LANES)), full((seq, LANES)),
                  full((ATT_INNER, ATT_INNER)), full((LANES, LANES)),
                  full((1, ATT_INNER)), full((1, LANES))],
        out_specs=pl.BlockSpec((tq, ATT_INNER), lambda b, i: (b * nq + i, 0)),
        out_shape=jax.ShapeDtypeStruct((m, ATT_INNER), jnp.float32),
        scratch_shapes=[pltpu.VMEM((ATT_KV_HEADS, seq, LANES), jnp.bfloat16),
                        pltpu.VMEM((ATT_KV_HEADS, seq, LANES), jnp.bfloat16)],
        compiler_params=pltpu.CompilerParams(
            dimension_semantics=("parallel", "arbitrary"), vmem_limit_bytes=VMEM_LIMIT),
        name="gqa",
    )(proj, proj, proj, cos_t, sin_t, cos_t, sin_t, gq, gq[:LANES, :LANES],
      jnp.tile(q_w, ATT_Q_HEADS).reshape(1, -1), jnp.tile(k_w, ATT_KV_HEADS).reshape(1, -1))


def rope_lane_tables(seq):
    cos, sin = axial_rope_tables(seq)
    cos2 = jnp.repeat(cos, 2, axis=1)
    sin2 = jnp.repeat(sin, 2, axis=1) * jnp.tile(jnp.array([-1.0, 1.0], jnp.float32), ATT_HEAD_DIM // 2)
    return jnp.tile(cos2, (1, 2)), jnp.tile(sin2, (1, 2))


def head_mean_matrix():
    idx = jnp.arange(ATT_INNER) // ATT_HEAD_DIM
    return jnp.where(idx[:, None] == idx[None, :], 1.0 / ATT_HEAD_DIM, 0.0).astype(jnp.bfloat16)


LN_TM = 1024


def _layer_norm(x, g, b):
    mu = jnp.mean(x, axis=-1, keepdims=True)
    xc = x - mu
    var = jnp.mean(xc * xc, axis=-1, keepdims=True)
    return xc * lax.rsqrt(var + LN_EPS) * g + b


def _bf16_split(x):
    hi = x.astype(jnp.bfloat16)
    return hi, (x - hi.astype(jnp.float32)).astype(jnp.bfloat16)


def _mix_ln_router_kernel(ya_ref, yb_ref, x_ref, nw_ref, wa_ref, wb_ref, g_ref, b_ref, rwh_ref, rwl_ref,
                          x1_ref, x1b_ref, aff_ref, *, rms_ya):
    ya = ya_ref[...]
    if rms_ya:
        ya = ya * lax.rsqrt(jnp.mean(ya * ya, axis=-1, keepdims=True) + NORM_EPS) * nw_ref[...]
    mix = (jnp.dot(ya.astype(jnp.bfloat16), wa_ref[...], preferred_element_type=jnp.float32)
           + jnp.dot(yb_ref[...].astype(jnp.bfloat16), wb_ref[...], preferred_element_type=jnp.float32))
    x1 = _layer_norm(DEEPNORM_ALPHA * x_ref[...] + mix, g_ref[...], b_ref[...])
    x1_ref[...] = x1
    x1b_ref[...] = x1.astype(jnp.bfloat16)
    hi, lo = _bf16_split(x1)
    rwh = rwh_ref[...]
    lt = (lax.dot_general(rwh, hi, NT_DIMS, preferred_element_type=jnp.float32)
          + lax.dot_general(rwl_ref[...], hi, NT_DIMS, preferred_element_type=jnp.float32)
          + lax.dot_general(rwh, lo, NT_DIMS, preferred_element_type=jnp.float32))
    e = jnp.exp(lt - jnp.max(lt, axis=0, keepdims=True))
    aff_ref[0] = e / jnp.sum(e, axis=0, keepdims=True)


def mix_ln_router(ya, yb, x, w_out, ln_g, ln_b, router_w, bsz, seq, ya_norm_w=None):
    m, d = x.shape
    tm = LN_TM
    nt = seq // tm
    ka = ya.shape[1]
    rms_ya = ya_norm_w is not None
    nw = (ya_norm_w if rms_ya else jnp.ones((ka,), jnp.float32)).reshape(1, ka)
    wa = w_out[:ka].astype(jnp.bfloat16)
    wb = w_out[ka:].astype(jnp.bfloat16)
    rwh, rwl = _bf16_split(router_w.T)
    ne = router_w.shape[1]
    row = lambda w: pl.BlockSpec((tm, w), lambda i: (i, 0))
    full = lambda a: pl.BlockSpec(a.shape, lambda i: (0, 0))
    g2, b2 = ln_g.reshape(1, d), ln_b.reshape(1, d)
    return pl.pallas_call(
        functools.partial(_mix_ln_router_kernel, rms_ya=rms_ya),
        grid=(m // tm,),
        in_specs=[row(ka), row(yb.shape[1]), row(d), full(nw), full(wa), full(wb), full(g2), full(b2),
                  full(rwh), full(rwl)],
        out_specs=[row(d), row(d), pl.BlockSpec((1, ne, tm), lambda i: (i // nt, 0, i % nt))],
        out_shape=[jax.ShapeDtypeStruct((m, d), jnp.float32),
                   jax.ShapeDtypeStruct((m, d), jnp.bfloat16),
                   jax.ShapeDtypeStruct((bsz, ne, seq), jnp.float32)],
        compiler_params=pltpu.CompilerParams(
            dimension_semantics=("parallel",), vmem_limit_bytes=VMEM_LIMIT),
        name="mix_ln_router",
    )(ya, yb, x, nw, wa, wb, g2, b2, rwh, rwl)


ROUTE_BLK = 256


def _excl_cumsum_lanes(mask, tri):
    n = mask.shape[1]
    blk = tri.shape[0]
    run = jnp.zeros((mask.shape[0], 1), jnp.float32)
    outs = []
    for k in range(n // blk):
        mb = mask[:, k * blk:(k + 1) * blk]
        outs.append(jnp.dot(mb.astype(jnp.bfloat16), tri, preferred_element_type=jnp.float32) + run)
        run = run + jnp.sum(mb, axis=-1, keepdims=True)
    return jnp.concatenate(outs, axis=1)


def _route_kernel(aff_ref, tri_ref, pos_ref, gsel_ref, *, cap):
    a = aff_ref[0]
    capf = jnp.float32(cap)
    bits = jnp.zeros((a.shape[0], 1), jnp.int32)
    for bit in range(30, -1, -1):
        cand = bits | (1 << bit)
        cnt = jnp.sum(jnp.where(a >= pltpu.bitcast(cand, jnp.float32), 1.0, 0.0), axis=-1, keepdims=True)
        bits = jnp.where(cnt >= capf, cand, bits)
    thr = pltpu.bitcast(bits, jnp.float32)
    gt = jnp.where(a > thr, 1.0, 0.0)
    eq = jnp.where(a == thr, 1.0, 0.0)
    need = capf - jnp.sum(gt, axis=-1, keepdims=True)
    tri = tri_ref[...]
    sel = gt + eq * jnp.where(_excl_cumsum_lanes(eq, tri) < need, 1.0, 0.0)
    pos = _excl_cumsum_lanes(sel, tri)
    chosen = sel > 0.5
    pos_ref[0] = jnp.where(chosen, pos, -1.0)
    gsel_ref[0] = jnp.where(chosen, a, 0.0)


def route(aff, cap):
    bsz, ne, seq = aff.shape
    blk = min(ROUTE_BLK, seq)
    idx = jnp.arange(blk)
    tri = (idx[:, None] < idx[None, :]).astype(jnp.bfloat16)
    spec = pl.BlockSpec((1, ne, seq), lambda b: (b, 0, 0))
    return pl.pallas_call(
        functools.partial(_route_kernel, cap=cap),
        grid=(bsz,),
        in_specs=[spec, pl.BlockSpec((blk, blk), lambda b: (0, 0))],
        out_specs=[spec, spec],
        out_shape=[jax.ShapeDtypeStruct((bsz, ne, seq), jnp.float32),
                   jax.ShapeDtypeStruct((bsz, ne, seq), jnp.float32)],
        compiler_params=pltpu.CompilerParams(
            dimension_semantics=("parallel",), vmem_limit_bytes=VMEM_LIMIT),
        name="route",
    )(aff, tri)


GATHER_NE = 4


def _gather_kernel(pos_ref, gsel_ref, x_ref, o_ref, gs_ref):
    nge, cap, d = o_ref.shape
    seq = x_ref.shape[0]
    slot = lax.broadcasted_iota(jnp.int32, (cap, seq), 0).astype(jnp.float32)
    hots = []
    for e in range(nge):
        hit = pos_ref[0, e] == slot
        hots.append(jnp.where(hit, 1.0, 0.0).astype(jnp.bfloat16))
        gs_ref[e] = jnp.sum(jnp.where(hit, gsel_ref[0, e], 0.0), axis=-1, keepdims=True)
    rows = jnp.dot(jnp.concatenate(hots, axis=0), x_ref[...], preferred_element_type=jnp.float32)
    o_ref[...] = rows.astype(jnp.bfloat16).reshape(nge, cap, d)


def gather_tokens(pos, gsel, x1b, cap):
    bsz, ne, seq = pos.shape
    d = x1b.shape[1]
    nge = GATHER_NE
    sel = pl.BlockSpec((1, nge, 1, seq), lambda b, g: (b, g, 0, 0))
    return pl.pallas_call(
        _gather_kernel,
        grid=(bsz, ne // nge),
        in_specs=[sel, sel, pl.BlockSpec((seq, d), lambda b, g: (b, 0))],
        out_specs=[pl.BlockSpec((nge, cap, d), lambda b, g: (g, b, 0)),
                   pl.BlockSpec((nge, cap, 1), lambda b, g: (g, b, 0))],
        out_shape=[jax.ShapeDtypeStruct((ne, bsz * cap, d), jnp.bfloat16),
                   jax.ShapeDtypeStruct((ne, bsz * cap, 1), jnp.float32)],
        compiler_params=pltpu.CompilerParams(
            dimension_semantics=("parallel", "parallel"), vmem_limit_bytes=VMEM_LIMIT),
        name="gather_tokens",
    )(pos.reshape(bsz, ne, 1, seq), gsel.reshape(bsz, ne, 1, seq), x1b)


COMB_TR = 512


def _combine_ln_kernel(ys_ref, post_ref, x1_ref, g_ref, b_ref, o_ref):
    post = post_ref[0]
    ne, cap, d = ys_ref.shape
    lane = lax.broadcasted_iota(jnp.int32, (post.shape[0], cap), 1).astype(jnp.float32)
    hots = [jnp.where(post[:, e:e + 1] == lane, 1.0, 0.0).astype(jnp.bfloat16) for e in range(ne)]
    ffn = jnp.dot(jnp.concatenate(hots, axis=1), ys_ref[...].reshape(ne * cap, d),
                  preferred_element_type=jnp.float32)
    o_ref[...] = _layer_norm(DEEPNORM_ALPHA * x1_ref[...] + ffn, g_ref[...], b_ref[...])


def combine_ln(ys, post, x1, ln_g, ln_b, cap):
    bsz, seq, ne = post.shape
    m, d = x1.shape
    tr = min(COMB_TR, seq)
    nr = seq // tr
    g2, b2 = ln_g.reshape(1, d), ln_b.reshape(1, d)
    vec = pl.BlockSpec((1, d), lambda b, r: (0, 0))
    rows = pl.BlockSpec((tr, d), lambda b, r: (b * nr + r, 0))
    return pl.pallas_call(
        _combine_ln_kernel,
        grid=(bsz, nr),
        in_specs=[pl.BlockSpec((ne, cap, d), lambda b, r: (0, b, 0)),
                  pl.BlockSpec((1, tr, ne), lambda b, r: (b, r, 0)), rows, vec, vec],
        out_specs=rows,
        out_shape=jax.ShapeDtypeStruct((m, d), jnp.float32),
        compiler_params=pltpu.CompilerParams(
            dimension_semantics=("parallel", "parallel"), vmem_limit_bytes=VMEM_LIMIT),
        name="combine_ln",
    )(ys, post, x1, g2, b2)


def moe_block(ya, yb, x2d, w_out, ln1_g, ln1_b, router_w, w1, w3, w2, layer, ln2_g, ln2_b, bsz, seq,
              ya_norm_w=None):
    cap = CAPACITY_FACTOR * seq // N_EXPERTS
    x1, x1b, aff = mix_ln_router(ya, yb, x2d, w_out, ln1_g, ln1_b, router_w, bsz, seq, ya_norm_w)
    pos, gsel = route(aff, cap)
    xs, gs = gather_tokens(pos, gsel, x1b, cap)
    ys = expert_ffn(xs, gs, w1, w3, w2, layer)
    return combine_ln(ys, pos.transpose(0, 2, 1), x1, ln2_g, ln2_b, cap)


SSD_GH = SSD_HEADS // SSD_GROUPS
SSD_GW = SSD_GH * SSD_HEAD_DIM
SSD_UNROLL = 2


def _shift_rows(x, s, row):
    n = x.shape[0]
    if s == 0:
        return x
    y = pltpu.roll(x, (-s) % n, axis=0)
    return jnp.where(row >= -s, y, 0.0) if s < 0 else jnp.where(row < n - s, y, 0.0)


def _conv_silu(x, w_ref, b_ref):
    row = lax.broadcasted_iota(jnp.int32, x.shape, 0)
    pad = (SSD_CONV_K - 1) // 2
    y = b_ref[...] + jnp.zeros_like(x)
    for k in range(SSD_CONV_K):
        y = y + w_ref[k:k + 1, :] * _shift_rows(x, k - pad, row)
    return y * jax.nn.sigmoid(y)


def _cumsum_rows(x, row, reverse):
    n = x.shape[0]
    s = 1
    while s < n:
        if reverse:
            x = x + jnp.where(row + s < n, pltpu.roll(x, n - s, axis=0), 0.0)
        else:
            x = x + jnp.where(row >= s, pltpu.roll(x, s, axis=0), 0.0)
        s *= 2
    return x


def _select_lanes(v, sel):
    hi = v.astype(jnp.bfloat16)
    r1 = v - hi.astype(jnp.float32)
    mid = r1.astype(jnp.bfloat16)
    lo = (r1 - mid.astype(jnp.float32)).astype(jnp.bfloat16)
    pieces = jnp.concatenate([hi, mid, lo], axis=1)
    return jnp.dot(pieces, jnp.concatenate([sel, sel, sel], axis=0), preferred_element_type=jnp.float32)


def _ssd_chunk(ux, ub, uc, dtv, la, st_ref, sele, reverse):
    q = SSD_CHUNK
    bf = jnp.bfloat16
    cols = [SSD_GH + h if reverse else h for h in range(SSD_GH)]
    row = lax.broadcasted_iota(jnp.int32, (q, LANES), 0)
    lane = lax.broadcasted_iota(jnp.int32, (q, LANES), 1)
    lane_lo = lane < SSD_HEAD_DIM
    tri = (row <= lane) if reverse else (row >= lane)
    cum = _cumsum_rows(la, row, reverse)
    cum_t = cum.T
    edge = 0 if reverse else q - 1
    cum_e = _select_lanes(cum, sele)
    tot_e = cum_e[edge:edge + 1, :]
    ub16 = ub.astype(bf)
    uc16 = uc.astype(bf)
    cb = lax.dot_general(uc16, ub16, NT_DIMS, preferred_element_type=jnp.float32)
    xdt = ux * _select_lanes(dtv, sele)
    ys = []
    for p in range(SSD_GH // 2):
        xp = xdt[:, p * LANES:(p + 1) * LANES]
        acc = None
        for half in range(2):
            c = cols[2 * p + half]
            seg = cum[:, c:c + 1] - cum_t[c:c + 1, :]
            lm = jnp.where(tri, jnp.exp(seg), 0.0)
            xh = jnp.where(lane_lo, xp, 0.0) if half == 0 else jnp.where(lane_lo, 0.0, xp)
            t = jnp.dot((cb * lm).astype(bf), xh.astype(bf), preferred_element_type=jnp.float32)
            acc = t if acc is None else acc + t
        ys.append(acc)
    y = jnp.concatenate(ys, axis=1)
    st = st_ref[...]
    y = y + jnp.exp(cum_e) * jnp.dot(uc16, st.astype(bf), preferred_element_type=jnp.float32)
    xe = (xdt * jnp.exp(tot_e - cum_e)).astype(bf)
    new = lax.dot_general(ub16, xe, (((0,), (0,)), ((), ())), preferred_element_type=jnp.float32)
    st_ref[...] = st * jnp.exp(tot_e) + new
    return y


def _ssd_kernel(z_ref, x_ref, b_ref, c_ref, dt_ref, cwx_ref, cwb_ref, cwc_ref, cbx_ref, cbb_ref, cbc_ref,
                dtb_ref, aneg_ref, dsk_ref, sele_ref, o_ref, ux_ref, ub_ref, uc_ref, dtv_ref, la_ref,
                yf_ref, yb_ref, stf_ref, stb_ref):
    ux_ref[...] = _conv_silu(x_ref[...], cwx_ref, cbx_ref)
    ub_ref[...] = _conv_silu(b_ref[...], cwb_ref, cbb_ref)
    uc_ref[...] = _conv_silu(c_ref[...], cwc_ref, cbc_ref)
    t = dt_ref[...] + dtb_ref[...]
    dtv = jnp.maximum(t, 0.0) + jnp.log1p(jnp.exp(-jnp.abs(t)))
    dtv_ref[...] = dtv
    la_ref[...] = dtv * aneg_ref[...]
    stf_ref[...] = jnp.zeros_like(stf_ref)
    stb_ref[...] = jnp.zeros_like(stb_ref)
    q = SSD_CHUNK
    nchunk = x_ref.shape[0] // q

    def body(n, carry):
        for u in range(SSD_UNROLL):
            c = n * SSD_UNROLL + u
            rf = pl.ds(pl.multiple_of(c * q, q), q)
            rb = pl.ds(pl.multiple_of((nchunk - 1 - c) * q, q), q)
            yf_ref[rf, :] = _ssd_chunk(ux_ref[rf, :], ub_ref[rf, :], uc_ref[rf, :], dtv_ref[rf, :], la_ref[rf, :],
                                       stf_ref, sele_ref[0], False)
            yb_ref[rb, :] = _ssd_chunk(ux_ref[rb, :], ub_ref[rb, :], uc_ref[rb, :], dtv_ref[rb, :], la_ref[rb, :],
                                       stb_ref, sele_ref[1], True)
        return carry

    lax.fori_loop(0, nchunk // SSD_UNROLL, body, 0)
    z = z_ref[...]
    y = yf_ref[...] + yb_ref[...] + ux_ref[...] * dsk_ref[...]
    o_ref[...] = y * (z * jax.nn.sigmoid(z))


def ssd(proj, dt, conv_w, conv_b, dt_bias_g, a_neg_g, d_skip_g, bsz, seq):
    m = proj.shape[0]
    gw, n = SSD_GW, SSD_STATE
    f32 = jnp.float32
    kk = SSD_CONV_K
    cb2 = conv_b.reshape(1, -1)
    in_specs = [
        pl.BlockSpec((seq, gw), lambda b, g: (b, g)),
        pl.BlockSpec((seq, gw), lambda b, g: (b, SSD_INNER // gw + g)),
        pl.BlockSpec((seq, n), lambda b, g: (b, 2 * SSD_INNER // n + g)),
        pl.BlockSpec((seq, n), lambda b, g: (b, 2 * SSD_INNER // n + SSD_GROUPS + g)),
        pl.BlockSpec((seq, LANES), lambda b, g: (b, g)),
        pl.BlockSpec((kk, gw), lambda b, g: (0, g)),
        pl.BlockSpec((kk, n), lambda b, g: (0, SSD_INNER // n + g)),
        pl.BlockSpec((kk, n), lambda b, g: (0, SSD_INNER // n + SSD_GROUPS + g)),
        pl.BlockSpec((1, gw), lambda b, g: (0, g)),
        pl.BlockSpec((1, n), lambda b, g: (0, SSD_INNER // n + g)),
        pl.BlockSpec((1, n), lambda b, g: (0, SSD_INNER // n + SSD_GROUPS + g)),
        pl.BlockSpec((1, LANES), lambda b, g: (0, g)),
        pl.BlockSpec((1, LANES), lambda b, g: (0, g)),
        pl.BlockSpec((1, gw), lambda b, g: (0, g)),
        pl.BlockSpec((2, LANES, gw), lambda b, g: (0, 0, 0)),
    ]
    src = jnp.arange(LANES)
    sele = jnp.stack([(src[:, None] == d * SSD_GH + jnp.arange(gw)[None, :] // SSD_HEAD_DIM) for d in range(2)])
    return pl.pallas_call(
        _ssd_kernel,
        grid=(bsz, SSD_GROUPS),
        in_specs=in_specs,
        out_specs=pl.BlockSpec((seq, gw), lambda b, g: (b, g)),
        out_shape=jax.ShapeDtypeStruct((m, SSD_INNER), f32),
        scratch_shapes=[pltpu.VMEM((seq, gw), f32), pltpu.VMEM((seq, n), f32), pltpu.VMEM((seq, n), f32),
                        pltpu.VMEM((seq, LANES), f32), pltpu.VMEM((seq, LANES), f32),
                        pltpu.VMEM((seq, gw), f32), pltpu.VMEM((seq, gw), f32),
                        pltpu.VMEM((n, gw), f32), pltpu.VMEM((n, gw), f32)],
        compiler_params=pltpu.CompilerParams(
            dimension_semantics=("parallel", "parallel"), vmem_limit_bytes=VMEM_LIMIT),
        name="ssd",
    )(proj, proj, proj, proj, dt, conv_w, conv_w, conv_w, cb2, cb2, cb2, dt_bias_g, a_neg_g, d_skip_g,
      sele.astype(jnp.bfloat16))


def _group_lanes(v):
    out = jnp.zeros((SSD_GROUPS, LANES), v.dtype)
    for g in range(SSD_GROUPS):
        out = out.at[g, :SSD_GH].set(v[0, g * SSD_GH:(g + 1) * SSD_GH])
        out = out.at[g, SSD_GH:2 * SSD_GH].set(v[1, g * SSD_GH:(g + 1) * SSD_GH])
    return out.reshape(1, -1)


def _dt_weight(w_dt):
    d = w_dt.shape[0]
    out = jnp.zeros((d, SSD_GROUPS, LANES), w_dt.dtype)
    for g in range(SSD_GROUPS):
        out = out.at[:, g, :SSD_GH].set(w_dt[:, g * SSD_GH:(g + 1) * SSD_GH])
        out = out.at[:, g, SSD_GH:2 * SSD_GH].set(w_dt[:, SSD_HEADS + g * SSD_GH:SSD_HEADS + (g + 1) * SSD_GH])
    return out.reshape(d, -1)


def _dt_proj_kernel(x_ref, wh_ref, wl_ref, o_ref):
    hi, lo = _bf16_split(x_ref[...])
    wh = wh_ref[...]
    o_ref[...] = (jnp.dot(hi, wh, preferred_element_type=jnp.float32)
                  + jnp.dot(hi, wl_ref[...], preferred_element_type=jnp.float32)
                  + jnp.dot(lo, wh, preferred_element_type=jnp.float32))


def dt_proj(x, w_dt_g):
    m, k = x.shape
    n = w_dt_g.shape[1]
    wh, wl = _bf16_split(w_dt_g)
    tm = LN_TM
    return pl.pallas_call(
        _dt_proj_kernel,
        grid=(m // tm,),
        in_specs=[pl.BlockSpec((tm, k), lambda i: (i, 0)),
                  pl.BlockSpec((k, n), lambda i: (0, 0)), pl.BlockSpec((k, n), lambda i: (0, 0))],
        out_specs=pl.BlockSpec((tm, n), lambda i: (i, 0)),
        out_shape=jax.ShapeDtypeStruct((m, n), jnp.float32),
        compiler_params=pltpu.CompilerParams(
            dimension_semantics=("parallel",), vmem_limit_bytes=VMEM_LIMIT),
        name="dt_proj",
    )(x, wh, wl)


def _pool_kernel(u_ref, w_ref, sc_ref, o_ref):
    n = u_ref.shape[0]
    row = lax.broadcasted_iota(jnp.int32, (n, POOL_GROUP), 0)
    for gi, win in enumerate(POOL_WINDOWS):
        u = u_ref[:, gi * POOL_GROUP:(gi + 1) * POOL_GROUP]
        half = win // 2
        fwd, bwd = u, u
        s = 1
        while s < half:
            fwd = fwd + _shift_rows(fwd, s, row)
            bwd = bwd + _shift_rows(bwd, -s, row)
            s *= 2
        total = fwd + _shift_rows(bwd, -1, row)
        cnt = (jnp.minimum(row + half, n) - jnp.maximum(row - half, 0)).astype(jnp.float32)
        pooled = total / cnt - u
        mixed = jnp.dot(pooled.astype(jnp.bfloat16), w_ref[gi], preferred_element_type=jnp.float32)
        o_ref[:, gi * POOL_GROUP:(gi + 1) * POOL_GROUP] = mixed * sc_ref[:, gi * POOL_GROUP:(gi + 1) * POOL_GROUP]


def multiscale_pool_kernel(proj, pool_w, pool_scale, bsz, seq):
    m = proj.shape[0]
    return pl.pallas_call(
        _pool_kernel,
        grid=(bsz,),
        in_specs=[pl.BlockSpec((seq, POOL_INNER), lambda b: (b, 0)),
                  pl.BlockSpec(pool_w.shape, lambda b: (0, 0, 0)),
                  pl.BlockSpec((1, POOL_INNER), lambda b: (0, 0))],
        out_specs=pl.BlockSpec((seq, POOL_INNER), lambda b: (b, 0)),
        out_shape=jax.ShapeDtypeStruct((m, POOL_INNER), jnp.float32),
        compiler_params=pltpu.CompilerParams(
            dimension_semantics=("parallel",), vmem_limit_bytes=VMEM_LIMIT),
        name="multiscale_pool",
    )(proj, pool_w.astype(jnp.bfloat16), pool_scale.reshape(1, -1))


def axial_rope_tables(seq):
    rows = seq // GRID_W
    row = jnp.repeat(jnp.arange(rows), GRID_W).astype(jnp.float32)
    col = jnp.tile(jnp.arange(GRID_W), rows).astype(jnp.float32)
    axis_dims = ATT_HEAD_DIM // 2
    freqs = ROPE_THETA ** (-jnp.arange(0, axis_dims, 2, dtype=jnp.float32) / axis_dims)
    ang = jnp.concatenate([row[:, None] * freqs, col[:, None] * freqs], axis=-1)
    return jnp.cos(ang), jnp.sin(ang)


def kernel(x, w_in_ab, ssm_conv_w, ssm_conv_b, ssm_dt_bias, ssm_a_log, ssm_d, ssm_norm_w, attn_q_norm, attn_k_norm, w_out_ab, w_in_cd, pool_w, pool_scale, hgrn_lb_logits, hgrn_norm_w, w_out_cd, router_w, moe_w1, moe_w3, moe_w2, ln1_g, ln1_b, ln2_g, ln2_b):
    bsz, seq, d = x.shape
    m = bsz * seq
    cos_t, sin_t = rope_lane_tables(seq)
    gq = head_mean_matrix()
    lb_all = jnp.cumsum(jax.nn.softmax(hgrn_lb_logits, axis=0), axis=0)
    lb_all = lb_all - lb_all[0]
    x2 = x.reshape(m, d)
    for layer in range(DEPTH):
        j = layer // 2
        if layer % 2 == 0:
            w = w_in_ab[j]
            w_main = jnp.concatenate([w[:, :1536], w[:, 1552:]], axis=1).astype(jnp.bfloat16)
            proj = matmul(x2, w_main)
            dt = dt_proj(x2, _dt_weight(w[:, 1536:1552]))
            ya = ssd(proj, dt, ssm_conv_w[j], ssm_conv_b[j], _group_lanes(ssm_dt_bias[j]),
                     _group_lanes(-jnp.exp(ssm_a_log[j])), jnp.repeat(ssm_d[j], SSD_HEAD_DIM).reshape(1, -1),
                     bsz, seq)
            yb = gqa(proj, cos_t, sin_t, gq, attn_q_norm[j], attn_k_norm[j], bsz, seq)
            w_out, ya_norm_w = w_out_ab[j], ssm_norm_w[j]
        else:
            proj = matmul(x2, w_in_cd[j].astype(jnp.bfloat16))
            ya = multiscale_pool_kernel(proj, pool_w[j], pool_scale[j], bsz, seq)
            yb = hgrn2(proj, lb_all[layer], hgrn_norm_w[j], bsz, seq)
            w_out, ya_norm_w = w_out_cd[j], None
        x2 = moe_block(ya, yb, x2, w_out, ln1_g[layer], ln1_b[layer], router_w[layer],
                       moe_w1, moe_w3, moe_w2, layer, ln2_g[layer], ln2_b[layer], bsz, seq, ya_norm_w)
    return x2.reshape(bsz, seq, d)
```

```python
import functools
import math

import jax
import jax.numpy as jnp
from jax import lax
from jax.experimental import pallas as pl
from jax.experimental.pallas import tpu as pltpu

DEPTH = 4
GRID_W = 64
NORM_EPS = 1e-6
LN_EPS = 1e-5

SSD_HEADS = 8
SSD_HEAD_DIM = 64
SSD_INNER = 512
SSD_GROUPS = 2
SSD_STATE = 128
SSD_CONV_K = 5
SSD_CHUNK = 128

ATT_Q_HEADS = 8
ATT_KV_HEADS = 2
ATT_HEAD_DIM = 64
ATT_INNER = 512
ROPE_THETA = 10000.0

POOL_WINDOWS = (2, 4, 8, 16)
POOL_GROUP = 128
POOL_INNER = 512

HGRN_HEADS = 4
HGRN_HEAD_DIM = 128
HGRN_INNER = 512

N_EXPERTS = 16
CAPACITY_FACTOR = 2

DEEPNORM_ALPHA = (2 * DEPTH) ** 0.25

LANES = 128
VMEM_LIMIT = 56 * 1024 * 1024
NT_DIMS = (((1,), (1,)), ((), ()))


def _mm_kernel(x_ref, w_ref, o_ref):
    o_ref[...] = jnp.dot(x_ref[...].astype(jnp.bfloat16), w_ref[...],
                         preferred_element_type=jnp.float32)


MM_TM = 2048
MM_TN = 768


def matmul(x, w_bf16, *, tm=MM_TM, tn=MM_TN):
    m, k = x.shape
    n = w_bf16.shape[1]
    return pl.pallas_call(
        _mm_kernel,
        grid=(m // tm, n // tn),
        in_specs=[pl.BlockSpec((tm, k), lambda i, j: (i, 0)),
                  pl.BlockSpec((k, tn), lambda i, j: (0, j))],
        out_specs=pl.BlockSpec((tm, tn), lambda i, j: (i, j)),
        out_shape=jax.ShapeDtypeStruct((m, n), jnp.float32),
        compiler_params=pltpu.CompilerParams(
            dimension_semantics=("parallel", "parallel"), vmem_limit_bytes=VMEM_LIMIT),
        name="matmul",
    )(x, w_bf16)


FFN_TF = 512


FFN_ROWS = 512


def _ffn_kernel(xs_ref, gs_ref, w1_ref, w3_ref, w2_ref, o_ref, acc_ref):
    f = pl.program_id(1)

    @pl.when(f == 0)
    def _():
        acc_ref[...] = jnp.zeros_like(acc_ref)

    w1 = w1_ref[...].astype(jnp.bfloat16)
    w3 = w3_ref[...].astype(jnp.bfloat16)
    w2 = w2_ref[...].astype(jnp.bfloat16)
    rc = min(FFN_ROWS, xs_ref.shape[0])
    for c in range(xs_ref.shape[0] // rc):
        rows = slice(c * rc, (c + 1) * rc)
        xc = xs_ref[rows, :]
        h1 = jnp.dot(xc, w1, preferred_element_type=jnp.float32)
        h3 = jnp.dot(xc, w3, preferred_element_type=jnp.float32)
        hdn = (h1 * jax.nn.sigmoid(h1) * h3).astype(jnp.bfloat16)
        acc_ref[rows, :] += jnp.dot(hdn, w2, preferred_element_type=jnp.float32)

    @pl.when(f == pl.num_programs(1) - 1)
    def _():
        o_ref[...] = (acc_ref[...] * gs_ref[...]).astype(o_ref.dtype)


def expert_ffn(xs, gs, w1, w3, w2, layer):
    e, r, d = xs.shape
    ff = w1.shape[3]
    tf = FFN_TF
    return pl.pallas_call(
        _ffn_kernel,
        grid=(e, ff // tf),
        in_specs=[pl.BlockSpec((None, r, d), lambda i, j: (i, 0, 0)),
                  pl.BlockSpec((None, r, 1), lambda i, j: (i, 0, 0)),
                  pl.BlockSpec((None, None, d, tf), lambda i, j: (layer, i, 0, j)),
                  pl.BlockSpec((None, None, d, tf), lambda i, j: (layer, i, 0, j)),
                  pl.BlockSpec((None, None, tf, d), lambda i, j: (layer, i, j, 0))],
        out_specs=pl.BlockSpec((None, r, d), lambda i, j: (i, 0, 0)),
        out_shape=jax.ShapeDtypeStruct((e, r, d), jnp.bfloat16),
        scratch_shapes=[pltpu.VMEM((r, d), jnp.float32)],
        compiler_params=pltpu.CompilerParams(
            dimension_semantics=("parallel", "arbitrary"), vmem_limit_bytes=VMEM_LIMIT),
        name="expert_ffn",
    )(xs, gs, w1, w3, w2)


HG_T = 64
HG_UNROLL = 4


def _hgrn_gate(raw, log_lb, log1m_lb, one_m_lb):
    e = jnp.exp(-jnp.abs(raw))
    den = 1.0 + e
    ls = jnp.minimum(raw, 0.0) - jnp.log(den)
    bb = log1m_lb + ls
    mx = jnp.maximum(log_lb, bb)
    logf = mx + jnp.log(1.0 + jnp.exp(-jnp.abs(log_lb - bb)))
    sig_neg = jnp.where(raw >= 0.0, e, 1.0) / den
    return logf * math.log2(math.e), one_m_lb * sig_neg


HG_S = 8


def _hgrn_block(q, k, lf, v, st_ref, backward):
    t, s8 = HG_T, HG_S
    hd = HGRN_HEAD_DIM
    bf = jnp.bfloat16
    row8 = lax.broadcasted_iota(jnp.int32, (s8, hd), 0)
    r8 = lax.broadcasted_iota(jnp.int32, (s8, t), 0)
    lt = lax.broadcasted_iota(jnp.int32, (s8, t), 1)
    sdiag, cloc = [], []
    for i in range(t // s8):
        rows = slice(i * s8, (i + 1) * s8)
        qh, kh_, lfh = q[rows], k[rows], lf[rows]
        w = jnp.zeros_like(qh)
        sc = jnp.zeros((s8, t), jnp.float32)
        for d in range(s8):
            sh = (s8 - d) % s8 if backward else d
            kd = kh_ if d == 0 else pltpu.roll(kh_, sh, axis=0)
            lfd = lfh if d == 0 else pltpu.roll(lfh, sh, axis=0)
            p = qh * kd * jnp.exp2(w)
            valid = (r8 + d <= s8 - 1) if backward else (r8 >= d)
            col = jnp.where(valid, (r8 + d if backward else r8 - d) + i * s8, -1)
            sc = jnp.where(lt == col, jnp.sum(p, axis=-1, keepdims=True), sc)
            w = w + lfd
        sdiag.append(sc)
        c = lfh
        s = 1
        while s < s8:
            if backward:
                c = c + jnp.where(row8 + s <= s8 - 1, pltpu.roll(c, s8 - s, axis=0), 0.0)
            else:
                c = c + jnp.where(row8 >= s, pltpu.roll(c, s, axis=0), 0.0)
            s *= 2
        cloc.append(c)
    scores = jnp.concatenate(sdiag, axis=0)
    cl = jnp.concatenate(cloc, axis=0)
    row = lax.broadcasted_iota(jnp.int32, (t, hd), 0)
    ti = lax.broadcasted_iota(jnp.int32, (t, t), 0)
    tj = lax.broadcasted_iota(jnp.int32, (t, t), 1)
    h = s8
    while h < t:
        is_far = ((row // h) % 2 == 0) if backward else ((row // h) % 2 == 1)
        parts = []
        for blk in range(t // (2 * h)):
            e = blk * 2 * h + (h if backward else h - 1)
            parts.append(jnp.broadcast_to(cl[e:e + 1, :], (2 * h, hd)))
        tn = jnp.concatenate(parts, axis=0) if len(parts) > 1 else parts[0]
        qt = jnp.where(is_far, q * jnp.exp2(cl), 0.0).astype(bf)
        kt = jnp.where(is_far, 0.0, k * jnp.exp2(jnp.where(is_far, 0.0, tn - cl))).astype(bf)
        s_h = lax.dot_general(qt, kt, NT_DIMS, preferred_element_type=jnp.float32)
        if 2 * h < t:
            s_h = jnp.where(ti // (2 * h) == tj // (2 * h), s_h, 0.0)
        scores = scores + s_h
        cl = cl + jnp.where(is_far, tn, 0.0)
        h *= 2
    edge = 0 if backward else t - 1
    tot = cl[edge:edge + 1, :]
    st = st_ref[...]
    vb = v.astype(bf)
    o = jnp.dot(scores.astype(bf), vb, preferred_element_type=jnp.float32)
    o = o + lax.dot_general((q * jnp.exp2(cl)).astype(bf), st.astype(bf), NT_DIMS,
                            preferred_element_type=jnp.float32)
    kh = (k * jnp.exp2(tot - cl)).astype(bf)
    ut = lax.dot_general(vb, kh, (((0,), (0,)), ((), ())), preferred_element_type=jnp.float32)
    st_ref[...] = st * jnp.exp2(tot) + ut
    return o


def _hgrn_kernel(q_ref, ff_ref, fb_ref, i_ref, g_ref, lb_ref, nw_ref, o_ref,
                 lff_ref, kf_ref, lfb_ref, kb_ref, of_ref, ob_ref, stf_ref, stb_ref):
    lb = lb_ref[...]
    log_lb = jnp.log(lb)
    log1m_lb = jnp.log1p(-lb)
    one_m_lb = 1.0 - lb
    lf, kk = _hgrn_gate(ff_ref[...], log_lb, log1m_lb, one_m_lb)
    lff_ref[...] = lf
    kf_ref[...] = kk
    lf, kk = _hgrn_gate(fb_ref[...], log_lb, log1m_lb, one_m_lb)
    lfb_ref[...] = lf
    kb_ref[...] = kk
    stf_ref[...] = jnp.zeros_like(stf_ref)
    stb_ref[...] = jnp.zeros_like(stb_ref)
    nblk = q_ref.shape[0] // HG_T

    def body(n, carry):
        for u in range(HG_UNROLL):
            blk = n * HG_UNROLL + u
            rf = pl.ds(pl.multiple_of(blk * HG_T, HG_T), HG_T)
            rb = pl.ds(pl.multiple_of((nblk - 1 - blk) * HG_T, HG_T), HG_T)
            of_ref[rf, :] = _hgrn_block(q_ref[rf, :], kf_ref[rf, :], lff_ref[rf, :], i_ref[rf, :], stf_ref, False)
            ob_ref[rb, :] = _hgrn_block(q_ref[rb, :], kb_ref[rb, :], lfb_ref[rb, :], i_ref[rb, :], stb_ref, True)
        return carry

    lax.fori_loop(0, nblk // HG_UNROLL, body, 0)
    o = of_ref[...] + ob_ref[...]
    o = o * lax.rsqrt(jnp.mean(o * o, axis=-1, keepdims=True) + NORM_EPS) * nw_ref[...]
    o_ref[...] = o * jax.nn.sigmoid(g_ref[...])


def hgrn2(proj, lb, norm_w, bsz, seq):
    m = proj.shape[0]
    hd = HGRN_HEAD_DIM
    nh = HGRN_HEADS

    def col(base):
        return pl.BlockSpec((seq, hd), lambda b, h: (b, base * nh + h))

    vec = pl.BlockSpec((1, hd), lambda b, h: (0, h))
    f32 = jnp.float32
    return pl.pallas_call(
        _hgrn_kernel,
        grid=(bsz, nh),
        in_specs=[col(1), col(2), col(3), col(4), col(5), vec, vec],
        out_specs=pl.BlockSpec((seq, hd), lambda b, h: (b, h)),
        out_shape=jax.ShapeDtypeStruct((m, HGRN_INNER), f32),
        scratch_shapes=[pltpu.VMEM((seq, hd), f32) for _ in range(6)]
        + [pltpu.VMEM((hd, hd), f32) for _ in range(2)],
        compiler_params=pltpu.CompilerParams(
            dimension_semantics=("parallel", "parallel"), vmem_limit_bytes=VMEM_LIMIT),
        name="hgrn2",
    )(proj, proj, proj, proj, proj, lb.reshape(1, -1), norm_w.reshape(1, -1))


ATT_TQ = 512
ATT_KC = 1024


def _group_mean_sq(x, g_ref):
    x2 = x * x
    hi = x2.astype(jnp.bfloat16)
    lo = (x2 - hi.astype(jnp.float32)).astype(jnp.bfloat16)
    g = g_ref[...]
    return (jnp.dot(hi, g, preferred_element_type=jnp.float32)
            + jnp.dot(lo, g, preferred_element_type=jnp.float32))


def _norm_rope(x, g_ref, w, cos, sin_signed):
    n = x.shape[1]
    y = x * lax.rsqrt(_group_mean_sq(x, g_ref) + NORM_EPS) * w
    lane = lax.broadcasted_iota(jnp.int32, y.shape, 1)
    partner = jnp.where(lane % 2 == 0, pltpu.roll(y, n - 1, axis=1), pltpu.roll(y, 1, axis=1))
    return y * cos + partner * sin_signed


def _dup_halves(x):
    lane = lax.broadcasted_iota(jnp.int32, x.shape, 1)
    sw = pltpu.roll(x, ATT_HEAD_DIM, axis=1)
    lo = lane < ATT_HEAD_DIM
    return jnp.where(lo, x, sw), jnp.where(lo, sw, x)


def _gqa_kernel(q_ref, k_ref, v_ref, cq_ref, sq_ref, ck_ref, sk_ref, gq_ref, gk_ref, qw_ref, kw_ref,
                o_ref, kd_ref, vd_ref):
    @pl.when(pl.program_id(1) == 0)
    def _():
        kr = _norm_rope(k_ref[...], gk_ref, kw_ref[...], ck_ref[...], sk_ref[...])
        k0, k1 = _dup_halves(kr)
        kd_ref[0] = k0.astype(jnp.bfloat16)
        kd_ref[1] = k1.astype(jnp.bfloat16)
        v0, v1 = _dup_halves(v_ref[...])
        vd_ref[0] = v0.astype(jnp.bfloat16)
        vd_ref[1] = v1.astype(jnp.bfloat16)

    tq = q_ref.shape[0]
    reps = ATT_INNER // LANES
    cos = jnp.concatenate([cq_ref[...]] * reps, axis=1)
    sin = jnp.concatenate([sq_ref[...]] * reps, axis=1)
    qr = _norm_rope(q_ref[...], gq_ref, qw_ref[...], cos, sin) * (ATT_HEAD_DIM ** -0.5)
    lane = lax.broadcasted_iota(jnp.int32, (tq, LANES), 1)
    lo = lane < ATT_HEAD_DIM
    rep = ATT_Q_HEADS // ATT_KV_HEADS
    for pair in range(ATT_Q_HEADS // 2):
        grp = (2 * pair) // rep
        qp = qr[:, pair * LANES:(pair + 1) * LANES]
        q2 = jnp.concatenate([jnp.where(lo, qp, 0.0), jnp.where(lo, 0.0, qp)], axis=0).astype(jnp.bfloat16)
        m = l = acc = None
        kc = min(ATT_KC, kd_ref.shape[1])
        for c in range(kd_ref.shape[1] // kc):
            keys = slice(c * kc, (c + 1) * kc)
            s = lax.dot_general(q2, kd_ref[grp, keys, :], NT_DIMS, preferred_element_type=jnp.float32)
            mc = jnp.max(s, axis=-1, keepdims=True)
            if c == 0:
                m = mc
                p = jnp.exp(s - m)
                l = jnp.sum(p, axis=-1, keepdims=True)
                acc = jnp.dot(p.astype(jnp.bfloat16), vd_ref[grp, keys, :], preferred_element_type=jnp.float32)
            else:
                m_new = jnp.maximum(m, mc)
                alpha = jnp.exp(m - m_new)
                p = jnp.exp(s - m_new)
                l = alpha * l + jnp.sum(p, axis=-1, keepdims=True)
                acc = alpha * acc + jnp.dot(p.astype(jnp.bfloat16), vd_ref[grp, keys, :],
                                            preferred_element_type=jnp.float32)
                m = m_new
        o2 = acc / l
        o_ref[:, pair * LANES:(pair + 1) * LANES] = jnp.where(lo, o2[:tq], o2[tq:])


def gqa(proj, cos_t, sin_t, gq, q_w, k_w, bsz, seq):
    m = proj.shape[0]
    tq = ATT_TQ
    nq = seq // tq
    kvw = ATT_KV_HEADS * ATT_HEAD_DIM
    full = lambda shape: pl.BlockSpec(shape, lambda b, i: (0, 0))
    return pl.pallas_call(
        _gqa_kernel,
        grid=(bsz, nq),
        in_specs=[pl.BlockSpec((tq, ATT_INNER), lambda b, i: (b * nq + i, 1536 // ATT_INNER)),
                  pl.BlockSpec((seq, kvw), lambda b, i: (b, 2048 // kvw)),
                  pl.BlockSpec((seq, kvw), lambda b, i: (b, 2176 // kvw)),
                  pl.BlockSpec((tq, LANES), lambda b, i: (i, 0)),
                  pl.BlockSpec((tq, LANES), lambda b, i: (i, 0)),
                  full((seq, LANES)), full((seq, LANES)),
                  full((ATT_INNER, ATT_INNER)), full((LANES, LANES)),
                  full((1, ATT_INNER)), full((1, LANES))],
        out_specs=pl.BlockSpec((tq, ATT_INNER), lambda b, i: (b * nq + i, 0)),
        out_shape=jax.ShapeDtypeStruct((m, ATT_INNER), jnp.float32),
        scratch_shapes=[pltpu.VMEM((ATT_KV_HEADS, seq, LANES), jnp.bfloat16),
                        pltpu.VMEM((ATT_KV_HEADS, seq, LANES), jnp.bfloat16)],
        compiler_params=pltpu.CompilerParams(
            dimension_semantics=("parallel", "arbitrary"), vmem_limit_bytes=VMEM_LIMIT),
        name="gqa",
    )(proj, proj, proj, cos_t, sin_t, cos_t, sin_t, gq, gq[:LANES, :LANES],
      jnp.tile(q_w, ATT_Q_HEADS).reshape(1, -1), jnp.tile(k_w, ATT_KV_HEADS).reshape(1, -1))


def rope_lane_tables(seq):
    cos, sin = axial_rope_tables(seq)
    cos2 = jnp.repeat(cos, 2, axis=1)
    sin2 = jnp.repeat(sin, 2, axis=1) * jnp.tile(jnp.array([-1.0, 1.0], jnp.float32), ATT_HEAD_DIM // 2)
    return jnp.tile(cos2, (1, 2)), jnp.tile(sin2, (1, 2))


def head_mean_matrix():
    idx = jnp.arange(ATT_INNER) // ATT_HEAD_DIM
    return jnp.where(idx[:, None] == idx[None, :], 1.0 / ATT_HEAD_DIM, 0.0).astype(jnp.bfloat16)


LN_TM = 1024


def _layer_norm(x, g, b):
    mu = jnp.mean(x, axis=-1, keepdims=True)
    xc = x - mu
    var = jnp.mean(xc * xc, axis=-1, keepdims=True)
    return xc * lax.rsqrt(var + LN_EPS) * g + b


def _bf16_split(x):
    hi = x.astype(jnp.bfloat16)
    return hi, (x - hi.astype(jnp.float32)).astype(jnp.bfloat16)


def _mix_ln_router_kernel(ya_ref, yb_ref, x_ref, nw_ref, wa_ref, wb_ref, g_ref, b_ref, rwh_ref, rwl_ref,
                          x1_ref, x1b_ref, aff_ref, *, rms_ya):
    ya = ya_ref[...]
    if rms_ya:
        ya = ya * lax.rsqrt(jnp.mean(ya * ya, axis=-1, keepdims=True) + NORM_EPS) * nw_ref[...]
    mix = (jnp.dot(ya.astype(jnp.bfloat16), wa_ref[...], preferred_element_type=jnp.float32)
           + jnp.dot(yb_ref[...].astype(jnp.bfloat16), wb_ref[...], preferred_element_type=jnp.float32))
    x1 = _layer_norm(DEEPNORM_ALPHA * x_ref[...] + mix, g_ref[...], b_ref[...])
    x1_ref[...] = x1
    x1b_ref[...] = x1.astype(jnp.bfloat16)
    hi, lo = _bf16_split(x1)
    rwh = rwh_ref[...]
    lt = (lax.dot_general(rwh, hi, NT_DIMS, preferred_element_type=jnp.float32)
          + lax.dot_general(rwl_ref[...], hi, NT_DIMS, preferred_element_type=jnp.float32)
          + lax.dot_general(rwh, lo, NT_DIMS, preferred_element_type=jnp.float32))
    e = jnp.exp(lt - jnp.max(lt, axis=0, keepdims=True))
    aff_ref[0] = e / jnp.sum(e, axis=0, keepdims=True)


def mix_ln_router(ya, yb, x, w_out, ln_g, ln_b, router_w, bsz, seq, ya_norm_w=None):
    m, d = x.shape
    tm = LN_TM
    nt = seq // tm
    ka = ya.shape[1]
    rms_ya = ya_norm_w is not None
    nw = (ya_norm_w if rms_ya else jnp.ones((ka,), jnp.float32)).reshape(1, ka)
    wa = w_out[:ka].astype(jnp.bfloat16)
    wb = w_out[ka:].astype(jnp.bfloat16)
    rwh, rwl = _bf16_split(router_w.T)
    ne = router_w.shape[1]
    row = lambda w: pl.BlockSpec((tm, w), lambda i: (i, 0))
    full = lambda a: pl.BlockSpec(a.shape, lambda i: (0, 0))
    g2, b2 = ln_g.reshape(1, d), ln_b.reshape(1, d)
    return pl.pallas_call(
        functools.partial(_mix_ln_router_kernel, rms_ya=rms_ya),
        grid=(m // tm,),
        in_specs=[row(ka), row(yb.shape[1]), row(d), full(nw), full(wa), full(wb), full(g2), full(b2),
                  full(rwh), full(rwl)],
        out_specs=[row(d), row(d), pl.BlockSpec((1, ne, tm), lambda i: (i // nt, 0, i % nt))],
        out_shape=[jax.ShapeDtypeStruct((m, d), jnp.float32),
                   jax.ShapeDtypeStruct((m, d), jnp.bfloat16),
                   jax.ShapeDtypeStruct((bsz, ne, seq), jnp.float32)],
        compiler_params=pltpu.CompilerParams(
            dimension_semantics=("parallel",), vmem_limit_bytes=VMEM_LIMIT),
        name="mix_ln_router",
    )(ya, yb, x, nw, wa, wb, g2, b2, rwh, rwl)


ROUTE_BLK = 256


def _excl_cumsum_lanes(mask, tri):
    n = mask.shape[1]
    blk = tri.shape[0]
    run = jnp.zeros((mask.shape[0], 1), jnp.float32)
    outs = []
    for k in range(n // blk):
        mb = mask[:, k * blk:(k + 1) * blk]
        outs.append(jnp.dot(mb.astype(jnp.bfloat16), tri, preferred_element_type=jnp.float32) + run)
        run = run + jnp.sum(mb, axis=-1, keepdims=True)
    return jnp.concatenate(outs, axis=1)


def _route_kernel(aff_ref, tri_ref, pos_ref, gsel_ref, *, cap):
    a = aff_ref[0]
    capf = jnp.float32(cap)
    bits = jnp.zeros((a.shape[0], 1), jnp.int32)
    for bit in range(30, -1, -1):
        cand = bits | (1 << bit)
        cnt = jnp.sum(jnp.where(a >= pltpu.bitcast(cand, jnp.float32), 1.0, 0.0), axis=-1, keepdims=True)
        bits = jnp.where(cnt >= capf, cand, bits)
    thr = pltpu.bitcast(bits, jnp.float32)
    gt = jnp.where(a > thr, 1.0, 0.0)
    eq = jnp.where(a == thr, 1.0, 0.0)
    need = capf - jnp.sum(gt, axis=-1, keepdims=True)
    tri = tri_ref[...]
    sel = gt + eq * jnp.where(_excl_cumsum_lanes(eq, tri) < need, 1.0, 0.0)
    pos = _excl_cumsum_lanes(sel, tri)
    chosen = sel > 0.5
    pos_ref[0] = jnp.where(chosen, pos, -1.0)
    gsel_ref[0] = jnp.where(chosen, a, 0.0)


def route(aff, cap):
    bsz, ne, seq = aff.shape
    blk = min(ROUTE_BLK, seq)
    idx = jnp.arange(blk)
    tri = (idx[:, None] < idx[None, :]).astype(jnp.bfloat16)
    spec = pl.BlockSpec((1, ne, seq), lambda b: (b, 0, 0))
    return pl.pallas_call(
        functools.partial(_route_kernel, cap=cap),
        grid=(bsz,),
        in_specs=[spec, pl.BlockSpec((blk, blk), lambda b: (0, 0))],
        out_specs=[spec, spec],
        out_shape=[jax.ShapeDtypeStruct((bsz, ne, seq), jnp.float32),
                   jax.ShapeDtypeStruct((bsz, ne, seq), jnp.float32)],
        compiler_params=pltpu.CompilerParams(
            dimension_semantics=("parallel",), vmem_limit_bytes=VMEM_LIMIT),
        name="route",
    )(aff, tri)


GATHER_NE = 4


def _gather_kernel(pos_ref, gsel_ref, x_ref, o_ref, gs_ref):
    nge, cap, d = o_ref.shape
    seq = x_ref.shape[0]
    slot = lax.broadcasted_iota(jnp.int32, (cap, seq), 0).astype(jnp.float32)
    hots = []
    for e in range(nge):
        hit = pos_ref[0, e] == slot
        hots.append(jnp.where(hit, 1.0, 0.0).astype(jnp.bfloat16))
        gs_ref[e] = jnp.sum(jnp.where(hit, gsel_ref[0, e], 0.0), axis=-1, keepdims=True)
    rows = jnp.dot(jnp.concatenate(hots, axis=0), x_ref[...], preferred_element_type=jnp.float32)
    o_ref[...] = rows.astype(jnp.bfloat16).reshape(nge, cap, d)


def gather_tokens(pos, gsel, x1b, cap):
    bsz, ne, seq = pos.shape
    d = x1b.shape[1]
    nge = GATHER_NE
    sel = pl.BlockSpec((1, nge, 1, seq), lambda b, g: (b, g, 0, 0))
    return pl.pallas_call(
        _gather_kernel,
        grid=(bsz, ne // nge),
        in_specs=[sel, sel, pl.BlockSpec((seq, d), lambda b, g: (b, 0))],
        out_specs=[pl.BlockSpec((nge, cap, d), lambda b, g: (g, b, 0)),
                   pl.BlockSpec((nge, cap, 1), lambda b, g: (g, b, 0))],
        out_shape=[jax.ShapeDtypeStruct((ne, bsz * cap, d), jnp.bfloat16),
                   jax.ShapeDtypeStruct((ne, bsz * cap, 1), jnp.float32)],
        compiler_params=pltpu.CompilerParams(
            dimension_semantics=("parallel", "parallel"), vmem_limit_bytes=VMEM_LIMIT),
        name="gather_tokens",
    )(pos.reshape(bsz, ne, 1, seq), gsel.reshape(bsz, ne, 1, seq), x1b)


COMB_TR = 512


def _combine_ln_kernel(ys_ref, post_ref, x1_ref, g_ref, b_ref, o_ref):
    post = post_ref[0]
    ne, cap, d = ys_ref.shape
    lane = lax.broadcasted_iota(jnp.int32, (post.shape[0], cap), 1).astype(jnp.float32)
    hots = [jnp.where(post[:, e:e + 1] == lane, 1.0, 0.0).astype(jnp.bfloat16) for e in range(ne)]
    ffn = jnp.dot(jnp.concatenate(hots, axis=1), ys_ref[...].reshape(ne * cap, d),
                  preferred_element_type=jnp.float32)
    o_ref[...] = _layer_norm(DEEPNORM_ALPHA * x1_ref[...] + ffn, g_ref[...], b_ref[...])


def combine_ln(ys, post, x1, ln_g, ln_b, cap):
    bsz, seq, ne = post.shape
    m, d = x1.shape
    tr = min(COMB_TR, seq)
    nr = seq // tr
    g2, b2 = ln_g.reshape(1, d), ln_b.reshape(1, d)
    vec = pl.BlockSpec((1, d), lambda b, r: (0, 0))
    rows = pl.BlockSpec((tr, d), lambda b, r: (b * nr + r, 0))
    return pl.pallas_call(
        _combine_ln_kernel,
        grid=(bsz, nr),
        in_specs=[pl.BlockSpec((ne, cap, d), lambda b, r: (0, b, 0)),
                  pl.BlockSpec((1, tr, ne), lambda b, r: (b, r, 0)), rows, vec, vec],
        out_specs=rows,
        out_shape=jax.ShapeDtypeStruct((m, d), jnp.float32),
        compiler_params=pltpu.CompilerParams(
            dimension_semantics=("parallel", "parallel"), vmem_limit_bytes=VMEM_LIMIT),
        name="combine_ln",
    )(ys, post, x1, g2, b2)


def moe_block(ya, yb, x2d, w_out, ln1_g, ln1_b, router_w, w1, w3, w2, layer, ln2_g, ln2_b, bsz, seq,
              ya_norm_w=None):
    cap = CAPACITY_FACTOR * seq // N_EXPERTS
    x1, x1b, aff = mix_ln_router(ya, yb, x2d, w_out, ln1_g, ln1_b, router_w, bsz, seq, ya_norm_w)
    pos, gsel = route(aff, cap)
    xs, gs = gather_tokens(pos, gsel, x1b, cap)
    ys = expert_ffn(xs, gs, w1, w3, w2, layer)
    return combine_ln(ys, pos.transpose(0, 2, 1), x1, ln2_g, ln2_b, cap)


SSD_GH = SSD_HEADS // SSD_GROUPS
SSD_GW = SSD_GH * SSD_HEAD_DIM
SSD_UNROLL = 2


def _shift_rows(x, s, row):
    n = x.shape[0]
    if s == 0:
        return x
    y = pltpu.roll(x, (-s) % n, axis=0)
    return jnp.where(row >= -s, y, 0.0) if s < 0 else jnp.where(row < n - s, y, 0.0)


def _conv_silu(x, w_ref, b_ref):
    row = lax.broadcasted_iota(jnp.int32, x.shape, 0)
    pad = (SSD_CONV_K - 1) // 2
    y = b_ref[...] + jnp.zeros_like(x)
    for k in range(SSD_CONV_K):
        y = y + w_ref[k:k + 1, :] * _shift_rows(x, k - pad, row)
    return y * jax.nn.sigmoid(y)


def _cumsum_rows(x, row, reverse):
    n = x.shape[0]
    s = 1
    while s < n:
        if reverse:
            x = x + jnp.where(row + s < n, pltpu.roll(x, n - s, axis=0), 0.0)
        else:
            x = x + jnp.where(row >= s, pltpu.roll(x, s, axis=0), 0.0)
        s *= 2
    return x


def _select_lanes(v, sel):
    hi = v.astype(jnp.bfloat16)
    r1 = v - hi.astype(jnp.float32)
    mid = r1.astype(jnp.bfloat16)
    lo = (r1 - mid.astype(jnp.float32)).astype(jnp.bfloat16)
    pieces = jnp.concatenate([hi, mid, lo], axis=1)
    return jnp.dot(pieces, jnp.concatenate([sel, sel, sel], axis=0), preferred_element_type=jnp.float32)


def _ssd_chunk(ux, ub, uc, dtv, la, st_ref, sele, reverse):
    q = SSD_CHUNK
    bf = jnp.bfloat16
    cols = [SSD_GH + h if reverse else h for h in range(SSD_GH)]
    row = lax.broadcasted_iota(jnp.int32, (q, LANES), 0)
    lane = lax.broadcasted_iota(jnp.int32, (q, LANES), 1)
    lane_lo = lane < SSD_HEAD_DIM
    tri = (row <= lane) if reverse else (row >= lane)
    cum = _cumsum_rows(la, row, reverse)
    cum_t = cum.T
    edge = 0 if reverse else q - 1
    cum_e = _select_lanes(cum, sele)
    tot_e = cum_e[edge:edge + 1, :]
    ub16 = ub.astype(bf)
    uc16 = uc.astype(bf)
    cb = lax.dot_general(uc16, ub16, NT_DIMS, preferred_element_type=jnp.float32)
    xdt = ux * _select_lanes(dtv, sele)
    ys = []
    for p in range(SSD_GH // 2):
        xp = xdt[:, p * LANES:(p + 1) * LANES]
        acc = None
        for half in range(2):
            c = cols[2 * p + half]
            seg = cum[:, c:c + 1] - cum_t[c:c + 1, :]
            lm = jnp.where(tri, jnp.exp(seg), 0.0)
            xh = jnp.where(lane_lo, xp, 0.0) if half == 0 else jnp.where(lane_lo, 0.0, xp)
            t = jnp.dot((cb * lm).astype(bf), xh.astype(bf), preferred_element_type=jnp.float32)
            acc = t if acc is None else acc + t
        ys.append(acc)
    y = jnp.concatenate(ys, axis=1)
    st = st_ref[...]
    y = y + jnp.exp(cum_e) * jnp.dot(uc16, st.astype(bf), preferred_element_type=jnp.float32)
    xe = (xdt * jnp.exp(tot_e - cum_e)).astype(bf)
    new = lax.dot_general(ub16, xe, (((0,), (0,)), ((), ())), preferred_element_type=jnp.float32)
    st_ref[...] = st * jnp.exp(tot_e) + new
    return y


def _ssd_kernel(z_ref, x_ref, b_ref, c_ref, dt_ref, cwx_ref, cwb_ref, cwc_ref, cbx_ref, cbb_ref, cbc_ref,
                dtb_ref, aneg_ref, dsk_ref, sele_ref, o_ref, ux_ref, ub_ref, uc_ref, dtv_ref, la_ref,
                yf_ref, yb_ref, stf_ref, stb_ref):
    ux_ref[...] = _conv_silu(x_ref[...], cwx_ref, cbx_ref)
    ub_ref[...] = _conv_silu(b_ref[...], cwb_ref, cbb_ref)
    uc_ref[...] = _conv_silu(c_ref[...], cwc_ref, cbc_ref)
    t = dt_ref[...] + dtb_ref[...]
    dtv = jnp.maximum(t, 0.0) + jnp.log1p(jnp.exp(-jnp.abs(t)))
    dtv_ref[...] = dtv
    la_ref[...] = dtv * aneg_ref[...]
    stf_ref[...] = jnp.zeros_like(stf_ref)
    stb_ref[...] = jnp.zeros_like(stb_ref)
    q = SSD_CHUNK
    nchunk = x_ref.shape[0] // q

    def body(n, carry):
        for u in range(SSD_UNROLL):
            c = n * SSD_UNROLL + u
            rf = pl.ds(pl.multiple_of(c * q, q), q)
            rb = pl.ds(pl.multiple_of((nchunk - 1 - c) * q, q), q)
            yf_ref[rf, :] = _ssd_chunk(ux_ref[rf, :], ub_ref[rf, :], uc_ref[rf, :], dtv_ref[rf, :], la_ref[rf, :],
                                       stf_ref, sele_ref[0], False)
            yb_ref[rb, :] = _ssd_chunk(ux_ref[rb, :], ub_ref[rb, :], uc_ref[rb, :], dtv_ref[rb, :], la_ref[rb, :],
                                       stb_ref, sele_ref[1], True)
        return carry

    lax.fori_loop(0, nchunk // SSD_UNROLL, body, 0)
    z = z_ref[...]
    y = yf_ref[...] + yb_ref[...] + ux_ref[...] * dsk_ref[...]
    o_ref[...] = y * (z * jax.nn.sigmoid(z))


def ssd(proj, dt, conv_w, conv_b, dt_bias_g, a_neg_g, d_skip_g, bsz, seq):
    m = proj.shape[0]
    gw, n = SSD_GW, SSD_STATE
    f32 = jnp.float32
    kk = SSD_CONV_K
    cb2 = conv_b.reshape(1, -1)
    in_specs = [
        pl.BlockSpec((seq, gw), lambda b, g: (b, g)),
        pl.BlockSpec((seq, gw), lambda b, g: (b, SSD_INNER // gw + g)),
        pl.BlockSpec((seq, n), lambda b, g: (b, 2 * SSD_INNER // n + g)),
        pl.BlockSpec((seq, n), lambda b, g: (b, 2 * SSD_INNER // n + SSD_GROUPS + g)),
        pl.BlockSpec((seq, LANES), lambda b, g: (b, g)),
        pl.BlockSpec((kk, gw), lambda b, g: (0, g)),
        pl.BlockSpec((kk, n), lambda b, g: (0, SSD_INNER // n + g)),
        pl.BlockSpec((kk, n), lambda b, g: (0, SSD_INNER // n + SSD_GROUPS + g)),
        pl.BlockSpec((1, gw), lambda b, g: (0, g)),
        pl.BlockSpec((1, n), lambda b, g: (0, SSD_INNER // n + g)),
        pl.BlockSpec((1, n), lambda b, g: (0, SSD_INNER // n + SSD_GROUPS + g)),
        pl.BlockSpec((1, LANES), lambda b, g: (0, g)),
        pl.BlockSpec((1, LANES), lambda b, g: (0, g)),
        pl.BlockSpec((1, gw), lambda b, g: (0, g)),
        pl.BlockSpec((2, LANES, gw), lambda b, g: (0, 0, 0)),
    ]
    src = jnp.arange(LANES)
    sele = jnp.stack([(src[:, None] == d * SSD_GH + jnp.arange(gw)[None, :] // SSD_HEAD_DIM) for d in range(2)])
    return pl.pallas_call(
        _ssd_kernel,
        grid=(bsz, SSD_GROUPS),
        in_specs=in_specs,
        out_specs=pl.BlockSpec((seq, gw), lambda b, g: (b, g)),
        out_shape=jax.ShapeDtypeStruct((m, SSD_INNER), f32),
        scratch_shapes=[pltpu.VMEM((seq, gw), f32), pltpu.VMEM((seq, n), f32), pltpu.VMEM((seq, n), f32),
                        pltpu.VMEM((seq, LANES), f32), pltpu.VMEM((seq, LANES), f32),
                        pltpu.VMEM((seq, gw), f32), pltpu.VMEM((seq, gw), f32),
                        pltpu.VMEM((n, gw), f32), pltpu.VMEM((n, gw), f32)],
        compiler_params=pltpu.CompilerParams(
            dimension_semantics=("parallel", "parallel"), vmem_limit_bytes=VMEM_LIMIT),
        name="ssd",
    )(proj, proj, proj, proj, dt, conv_w, conv_w, conv_w, cb2, cb2, cb2, dt_bias_g, a_neg_g, d_skip_g,
      sele.astype(jnp.bfloat16))


def _group_lanes(v):
    out = jnp.zeros((SSD_GROUPS, LANES), v.dtype)
    for g in range(SSD_GROUPS):
        out = out.at[g, :SSD_GH].set(v[0, g * SSD_GH:(g + 1) * SSD_GH])
        out = out.at[g, SSD_GH:2 * SSD_GH].set(v[1, g * SSD_GH:(g + 1) * SSD_GH])
    return out.reshape(1, -1)


def _dt_weight(w_dt):
    d = w_dt.shape[0]
    out = jnp.zeros((d, SSD_GROUPS, LANES), w_dt.dtype)
    for g in range(SSD_GROUPS):
        out = out.at[:, g, :SSD_GH].set(w_dt[:, g * SSD_GH:(g + 1) * SSD_GH])
        out = out.at[:, g, SSD_GH:2 * SSD_GH].set(w_dt[:, SSD_HEADS + g * SSD_GH:SSD_HEADS + (g + 1) * SSD_GH])
    return out.reshape(d, -1)


def _dt_proj_kernel(x_ref, wh_ref, wl_ref, o_ref):
    hi, lo = _bf16_split(x_ref[...])
    wh = wh_ref[...]
    o_ref[...] = (jnp.dot(hi, wh, preferred_element_type=jnp.float32)
                  + jnp.dot(hi, wl_ref[...], preferred_element_type=jnp.float32)
                  + jnp.dot(lo, wh, preferred_element_type=jnp.float32))


def dt_proj(x, w_dt_g):
    m, k = x.shape
    n = w_dt_g.shape[1]
    wh, wl = _bf16_split(w_dt_g)
    tm = LN_TM
    return pl.pallas_call(
        _dt_proj_kernel,
        grid=(m // tm,),
        in_specs=[pl.BlockSpec((tm, k), lambda i: (i, 0)),
                  pl.BlockSpec((k, n), lambda i: (0, 0)), pl.BlockSpec((k, n), lambda i: (0, 0))],
        out_specs=pl.BlockSpec((tm, n), lambda i: (i, 0)),
        out_shape=jax.ShapeDtypeStruct((m, n), jnp.float32),
        compiler_params=pltpu.CompilerParams(
            dimension_semantics=("parallel",), vmem_limit_bytes=VMEM_LIMIT),
        name="dt_proj",
    )(x, wh, wl)


def _pool_kernel(u_ref, w_ref, sc_ref, o_ref):
    n = u_ref.shape[0]
    row = lax.broadcasted_iota(jnp.int32, (n, POOL_GROUP), 0)
    for gi, win in enumerate(POOL_WINDOWS):
        u = u_ref[:, gi * POOL_GROUP:(gi + 1) * POOL_GROUP]
        half = win // 2
        fwd, bwd = u, u
        s = 1
        while s < half:
            fwd = fwd + _shift_rows(fwd, s, row)
            bwd = bwd + _shift_rows(bwd, -s, row)
            s *= 2
        total = fwd + _shift_rows(bwd, -1, row)
        cnt = (jnp.minimum(row + half, n) - jnp.maximum(row - half, 0)).astype(jnp.float32)
        pooled = total / cnt - u
        mixed = jnp.dot(pooled.astype(jnp.bfloat16), w_ref[gi], preferred_element_type=jnp.float32)
        o_ref[:, gi * POOL_GROUP:(gi + 1) * POOL_GROUP] = mixed * sc_ref[:, gi * POOL_GROUP:(gi + 1) * POOL_GROUP]


def multiscale_pool_kernel(proj, pool_w, pool_scale, bsz, seq):
    m = proj.shape[0]
    return pl.pallas_call(
        _pool_kernel,
        grid=(bsz,),
        in_specs=[pl.BlockSpec((seq, POOL_INNER), lambda b: (b, 0)),
                  pl.BlockSpec(pool_w.shape, lambda b: (0, 0, 0)),
                  pl.BlockSpec((1, POOL_INNER), lambda b: (0, 0))],
        out_specs=pl.BlockSpec((seq, POOL_INNER), lambda b: (b, 0)),
        out_shape=jax.ShapeDtypeStruct((m, POOL_INNER), jnp.float32),
        compiler_params=pltpu.CompilerParams(
            dimension_semantics=("parallel",), vmem_limit_bytes=VMEM_LIMIT),
        name="multiscale_pool",
    )(proj, pool_w.astype(jnp.bfloat16), pool_scale.reshape(1, -1))


def axial_rope_tables(seq):
    rows = seq // GRID_W
    row = jnp.repeat(jnp.arange(rows), GRID_W).astype(jnp.float32)
    col = jnp.tile(jnp.arange(GRID_W), rows).astype(jnp.float32)
    axis_dims = ATT_HEAD_DIM // 2
    freqs = ROPE_THETA ** (-jnp.arange(0, axis_dims, 2, dtype=jnp.float32) / axis_dims)
    ang = jnp.concatenate([row[:, None] * freqs, col[:, None] * freqs], axis=-1)
    return jnp.cos(ang), jnp.sin(ang)


def kernel(x, w_in_ab, ssm_conv_w, ssm_conv_b, ssm_dt_bias, ssm_a_log, ssm_d, ssm_norm_w, attn_q_norm, attn_k_norm, w_out_ab, w_in_cd, pool_w, pool_scale, hgrn_lb_logits, hgrn_norm_w, w_out_cd, router_w, moe_w1, moe_w3, moe_w2, ln1_g, ln1_b, ln2_g, ln2_b):
    bsz, seq, d = x.shape
    m = bsz * seq
    cos_t, sin_t = rope_lane_tables(seq)
    gq = head_mean_matrix()
    lb_all = jnp.cumsum(jax.nn.softmax(hgrn_lb_logits, axis=0), axis=0)
    lb_all = lb_all - lb_all[0]
    x2 = x.reshape(m, d)
    for layer in range(DEPTH):
        j = layer // 2
        if layer % 2 == 0:
            w = w_in_ab[j]
            w_main = jnp.concatenate([w[:, :1536], w[:, 1552:]], axis=1).astype(jnp.bfloat16)
            proj = matmul(x2, w_main)
            dt = dt_proj(x2, _dt_weight(w[:, 1536:1552]))
            ya = ssd(proj, dt, ssm_conv_w[j], ssm_conv_b[j], _group_lanes(ssm_dt_bias[j]),
                     _group_lanes(-jnp.exp(ssm_a_log[j])), jnp.repeat(ssm_d[j], SSD_HEAD_DIM).reshape(1, -1),
                     bsz, seq)
            yb = gqa(proj, cos_t, sin_t, gq, attn_q_norm[j], attn_k_norm[j], bsz, seq)
            w_out, ya_norm_w = w_out_ab[j], ssm_norm_w[j]
        else:
            proj = matmul(x2, w_in_cd[j].astype(jnp.bfloat16))
            ya = multiscale_pool_kernel(proj, pool_w[j], pool_scale[j], bsz, seq)
            yb = hgrn2(proj, lb_all[layer], hgrn_norm_w[j], bsz, seq)
            w_out, ya_norm_w = w_out_cd[j], None
        x2 = moe_block(ya, yb, x2, w_out, ln1_g[layer], ln1_b[layer], router_w[layer],
                       moe_w1, moe_w3, moe_w2, layer, ln2_g[layer], ln2_b[layer], bsz, seq, ya_norm_w)
    return x2.reshape(bsz, seq, d)
```

```python
import functools
import math

import jax
import jax.numpy as jnp
from jax import lax
from jax.experimental import pallas as pl
from jax.experimental.pallas import tpu as pltpu

DEPTH = 4
GRID_W = 64
NORM_EPS = 1e-6
LN_EPS = 1e-5

SSD_HEADS = 8
SSD_HEAD_DIM = 64
SSD_INNER = 512
SSD_GROUPS = 2
SSD_STATE = 128
SSD_CONV_K = 5
SSD_CHUNK = 128

ATT_Q_HEADS = 8
ATT_KV_HEADS = 2
ATT_HEAD_DIM = 64
ATT_INNER = 512
ROPE_THETA = 10000.0

POOL_WINDOWS = (2, 4, 8, 16)
POOL_GROUP = 128
POOL_INNER = 512

HGRN_HEADS = 4
HGRN_HEAD_DIM = 128
HGRN_INNER = 512

N_EXPERTS = 16
CAPACITY_FACTOR = 2

DEEPNORM_ALPHA = (2 * DEPTH) ** 0.25

LANES = 128
VMEM_LIMIT = 56 * 1024 * 1024
NT_DIMS = (((1,), (1,)), ((), ()))


def _mm_kernel(x_ref, w_ref, o_ref):
    o_ref[...] = jnp.dot(x_ref[...].astype(jnp.bfloat16), w_ref[...],
                         preferred_element_type=jnp.float32)


MM_TM = 2048
MM_TN = 768


def matmul(x, w_bf16, *, tm=MM_TM, tn=MM_TN):
    m, k = x.shape
    n = w_bf16.shape[1]
    return pl.pallas_call(
        _mm_kernel,
        grid=(m // tm, n // tn),
        in_specs=[pl.BlockSpec((tm, k), lambda i, j: (i, 0)),
                  pl.BlockSpec((k, tn), lambda i, j: (0, j))],
        out_specs=pl.BlockSpec((tm, tn), lambda i, j: (i, j)),
        out_shape=jax.ShapeDtypeStruct((m, n), jnp.float32),
        compiler_params=pltpu.CompilerParams(
            dimension_semantics=("parallel", "parallel"), vmem_limit_bytes=VMEM_LIMIT),
        name="matmul",
    )(x, w_bf16)


FFN_TF = 512


FFN_ROWS = 512


def _ffn_kernel(xs_ref, gs_ref, w1_ref, w3_ref, w2_ref, o_ref, acc_ref):
    f = pl.program_id(1)

    @pl.when(f == 0)
    def _():
        acc_ref[...] = jnp.zeros_like(acc_ref)

    w1 = w1_ref[...].astype(jnp.bfloat16)
    w3 = w3_ref[...].astype(jnp.bfloat16)
    w2 = w2_ref[...].astype(jnp.bfloat16)
    rc = min(FFN_ROWS, xs_ref.shape[0])
    for c in range(xs_ref.shape[0] // rc):
        rows = slice(c * rc, (c + 1) * rc)
        xc = xs_ref[rows, :]
        h1 = jnp.dot(xc, w1, preferred_element_type=jnp.float32)
        h3 = jnp.dot(xc, w3, preferred_element_type=jnp.float32)
        hdn = (h1 * jax.nn.sigmoid(h1) * h3).astype(jnp.bfloat16)
        acc_ref[rows, :] += jnp.dot(hdn, w2, preferred_element_type=jnp.float32)

    @pl.when(f == pl.num_programs(1) - 1)
    def _():
        o_ref[...] = (acc_ref[...] * gs_ref[...]).astype(o_ref.dtype)


def expert_ffn(xs, gs, w1, w3, w2, layer):
    e, r, d = xs.shape
    ff = w1.shape[3]
    tf = FFN_TF
    return pl.pallas_call(
        _ffn_kernel,
        grid=(e, ff // tf),
        in_specs=[pl.BlockSpec((None, r, d), lambda i, j: (i, 0, 0)),
                  pl.BlockSpec((None, r, 1), lambda i, j: (i, 0, 0)),
                  pl.BlockSpec((None, None, d, tf), lambda i, j: (layer, i, 0, j)),
                  pl.BlockSpec((None, None, d, tf), lambda i, j: (layer, i, 0, j)),
                  pl.BlockSpec((None, None, tf, d), lambda i, j: (layer, i, j, 0))],
        out_specs=pl.BlockSpec((None, r, d), lambda i, j: (i, 0, 0)),
        out_shape=jax.ShapeDtypeStruct((e, r, d), jnp.bfloat16),
        scratch_shapes=[pltpu.VMEM((r, d), jnp.float32)],
        compiler_params=pltpu.CompilerParams(
            dimension_semantics=("parallel", "arbitrary"), vmem_limit_bytes=VMEM_LIMIT),
        name="expert_ffn",
    )(xs, gs, w1, w3, w2)


HG_T = 128
HG_UNROLL = 2


def _hgrn_gate(raw, log_lb, log1m_lb, one_m_lb):
    e = jnp.exp(-jnp.abs(raw))
    den = 1.0 + e
    ls = jnp.minimum(raw, 0.0) - jnp.log(den)
    bb = log1m_lb + ls
    mx = jnp.maximum(log_lb, bb)
    logf = mx + jnp.log(1.0 + jnp.exp(-jnp.abs(log_lb - bb)))
    sig_neg = jnp.where(raw >= 0.0, e, 1.0) / den
    return logf * math.log2(math.e), one_m_lb * sig_neg


HG_S = 8


def _hgrn_block(q, k, lf, v, st_ref, backward):
    t, s8 = HG_T, HG_S
    hd = HGRN_HEAD_DIM
    bf = jnp.bfloat16
    row8 = lax.broadcasted_iota(jnp.int32, (s8, hd), 0)
    r8 = lax.broadcasted_iota(jnp.int32, (s8, t), 0)
    lt = lax.broadcasted_iota(jnp.int32, (s8, t), 1)
    sdiag, cloc = [], []
    for i in range(t // s8):
        rows = slice(i * s8, (i + 1) * s8)
        qh, kh_, lfh = q[rows], k[rows], lf[rows]
        w = jnp.zeros_like(qh)
        sc = jnp.zeros((s8, t), jnp.float32)
        for d in range(s8):
            sh = (s8 - d) % s8 if backward else d
            kd = kh_ if d == 0 else pltpu.roll(kh_, sh, axis=0)
            lfd = lfh if d == 0 else pltpu.roll(lfh, sh, axis=0)
            p = qh * kd * jnp.exp2(w)
            valid = (r8 + d <= s8 - 1) if backward else (r8 >= d)
            col = jnp.where(valid, (r8 + d if backward else r8 - d) + i * s8, -1)
            sc = jnp.where(lt == col, jnp.sum(p, axis=-1, keepdims=True), sc)
            w = w + lfd
        sdiag.append(sc)
        c = lfh
        s = 1
        while s < s8:
            if backward:
                c = c + jnp.where(row8 + s <= s8 - 1, pltpu.roll(c, s8 - s, axis=0), 0.0)
            else:
                c = c + jnp.where(row8 >= s, pltpu.roll(c, s, axis=0), 0.0)
            s *= 2
        cloc.append(c)
    scores = jnp.concatenate(sdiag, axis=0)
    cl = jnp.concatenate(cloc, axis=0)
    row = lax.broadcasted_iota(jnp.int32, (t, hd), 0)
    ti = lax.broadcasted_iota(jnp.int32, (t, t), 0)
    tj = lax.broadcasted_iota(jnp.int32, (t, t), 1)
    h = s8
    while h < t:
        is_far = ((row // h) % 2 == 0) if backward else ((row // h) % 2 == 1)
        parts = []
        for blk in range(t // (2 * h)):
            e = blk * 2 * h + (h if backward else h - 1)
            parts.append(jnp.broadcast_to(cl[e:e + 1, :], (2 * h, hd)))
        tn = jnp.concatenate(parts, axis=0) if len(parts) > 1 else parts[0]
        qt = jnp.where(is_far, q * jnp.exp2(cl), 0.0).astype(bf)
        kt = jnp.where(is_far, 0.0, k * jnp.exp2(jnp.where(is_far, 0.0, tn - cl))).astype(bf)
        s_h = lax.dot_general(qt, kt, NT_DIMS, preferred_element_type=jnp.float32)
        if 2 * h < t:
            s_h = jnp.where(ti // (2 * h) == tj // (2 * h), s_h, 0.0)
        scores = scores + s_h
        cl = cl + jnp.where(is_far, tn, 0.0)
        h *= 2
    edge = 0 if backward else t - 1
    tot = cl[edge:edge + 1, :]
    st = st_ref[...]
    vb = v.astype(bf)
    o = jnp.dot(scores.astype(bf), vb, preferred_element_type=jnp.float32)
    o = o + lax.dot_general((q * jnp.exp2(cl)).astype(bf), st.astype(bf), NT_DIMS,
                            preferred_element_type=jnp.float32)
    kh = (k * jnp.exp2(tot - cl)).astype(bf)
    ut = lax.dot_general(vb, kh, (((0,), (0,)), ((), ())), preferred_element_type=jnp.float32)
    st_ref[...] = st * jnp.exp2(tot) + ut
    return o


def _hgrn_kernel(q_ref, ff_ref, fb_ref, i_ref, g_ref, lb_ref, nw_ref, o_ref,
                 lff_ref, kf_ref, lfb_ref, kb_ref, of_ref, ob_ref, stf_ref, stb_ref):
    lb = lb_ref[...]
    log_lb = jnp.log(lb)
    log1m_lb = jnp.log1p(-lb)
    one_m_lb = 1.0 - lb
    lf, kk = _hgrn_gate(ff_ref[...], log_lb, log1m_lb, one_m_lb)
    lff_ref[...] = lf
    kf_ref[...] = kk
    lf, kk = _hgrn_gate(fb_ref[...], log_lb, log1m_lb, one_m_lb)
    lfb_ref[...] = lf
    kb_ref[...] = kk
    stf_ref[...] = jnp.zeros_like(stf_ref)
    stb_ref[...] = jnp.zeros_like(stb_ref)
    nblk = q_ref.shape[0] // HG_T

    def body(n, carry):
        for u in range(HG_UNROLL):
            blk = n * HG_UNROLL + u
            rf = pl.ds(pl.multiple_of(blk * HG_T, HG_T), HG_T)
            rb = pl.ds(pl.multiple_of((nblk - 1 - blk) * HG_T, HG_T), HG_T)
            of_ref[rf, :] = _hgrn_block(q_ref[rf, :], kf_ref[rf, :], lff_ref[rf, :], i_ref[rf, :], stf_ref, False)
            ob_ref[rb, :] = _hgrn_block(q_ref[rb, :], kb_ref[rb, :], lfb_ref[rb, :], i_ref[rb, :], stb_ref, True)
        return carry

    lax.fori_loop(0, nblk // HG_UNROLL, body, 0)
    o = of_ref[...] + ob_ref[...]
    o = o * lax.rsqrt(jnp.mean(o * o, axis=-1, keepdims=True) + NORM_EPS) * nw_ref[...]
    o_ref[...] = o * jax.nn.sigmoid(g_ref[...])


def hgrn2(proj, lb, norm_w, bsz, seq):
    m = proj.shape[0]
    hd = HGRN_HEAD_DIM
    nh = HGRN_HEADS

    def col(base):
        return pl.BlockSpec((seq, hd), lambda b, h: (b, base * nh + h))

    vec = pl.BlockSpec((1, hd), lambda b, h: (0, h))
    f32 = jnp.float32
    return pl.pallas_call(
        _hgrn_kernel,
        grid=(bsz, nh),
        in_specs=[col(1), col(2), col(3), col(4), col(5), vec, vec],
        out_specs=pl.BlockSpec((seq, hd), lambda b, h: (b, h)),
        out_shape=jax.ShapeDtypeStruct((m, HGRN_INNER), f32),
        scratch_shapes=[pltpu.VMEM((seq, hd), f32) for _ in range(6)]
        + [pltpu.VMEM((hd, hd), f32) for _ in range(2)],
        compiler_params=pltpu.CompilerParams(
            dimension_semantics=("parallel", "parallel"), vmem_limit_bytes=VMEM_LIMIT),
        name="hgrn2",
    )(proj, proj, proj, proj, proj, lb.reshape(1, -1), norm_w.reshape(1, -1))


ATT_TQ = 512
ATT_KC = 1024


def _group_mean_sq(x, g_ref):
    x2 = x * x
    hi = x2.astype(jnp.bfloat16)
    lo = (x2 - hi.astype(jnp.float32)).astype(jnp.bfloat16)
    g = g_ref[...]
    return (jnp.dot(hi, g, preferred_element_type=jnp.float32)
            + jnp.dot(lo, g, preferred_element_type=jnp.float32))


def _norm_rope(x, g_ref, w, cos, sin_signed):
    n = x.shape[1]
    y = x * lax.rsqrt(_group_mean_sq(x, g_ref) + NORM_EPS) * w
    lane = lax.broadcasted_iota(jnp.int32, y.shape, 1)
    partner = jnp.where(lane % 2 == 0, pltpu.roll(y, n - 1, axis=1), pltpu.roll(y, 1, axis=1))
    return y * cos + partner * sin_signed


def _dup_halves(x):
    lane = lax.broadcasted_iota(jnp.int32, x.shape, 1)
    sw = pltpu.roll(x, ATT_HEAD_DIM, axis=1)
    lo = lane < ATT_HEAD_DIM
    return jnp.where(lo, x, sw), jnp.where(lo, sw, x)


def _gqa_kernel(q_ref, k_ref, v_ref, cq_ref, sq_ref, ck_ref, sk_ref, gq_ref, gk_ref, qw_ref, kw_ref,
                o_ref, kd_ref, vd_ref):
    @pl.when(pl.program_id(1) == 0)
    def _():
        kr = _norm_rope(k_ref[...], gk_ref, kw_ref[...], ck_ref[...], sk_ref[...])
        k0, k1 = _dup_halves(kr)
        kd_ref[0] = k0.astype(jnp.bfloat16)
        kd_ref[1] = k1.astype(jnp.bfloat16)
        v0, v1 = _dup_halves(v_ref[...])
        vd_ref[0] = v0.astype(jnp.bfloat16)
        vd_ref[1] = v1.astype(jnp.bfloat16)

    tq = q_ref.shape[0]
    reps = ATT_INNER // LANES
    cos = jnp.concatenate([cq_ref[...]] * reps, axis=1)
    sin = jnp.concatenate([sq_ref[...]] * reps, axis=1)
    qr = _norm_rope(q_ref[...], gq_ref, qw_ref[...], cos, sin) * (ATT_HEAD_DIM ** -0.5)
    lane = lax.broadcasted_iota(jnp.int32, (tq, LANES), 1)
    lo = lane < ATT_HEAD_DIM
    rep = ATT_Q_HEADS // ATT_KV_HEADS
    for pair in range(ATT_Q_HEADS // 2):
        grp = (2 * pair) // rep
        qp = qr[:, pair * LANES:(pair + 1) * LANES]
        q2 = jnp.concatenate([jnp.where(lo, qp, 0.0), jnp.where(lo, 0.0, qp)], axis=0).astype(jnp.bfloat16)
        m = l = acc = None
        kc = min(ATT_KC, kd_ref.shape[1])
        for c in range(kd_ref.shape[1] // kc):
            keys = slice(c * kc, (c + 1) * kc)
            s = lax.dot_general(q2, kd_ref[grp, keys, :], NT_DIMS, preferred_element_type=jnp.float32)
            mc = jnp.max(s, axis=-1, keepdims=True)
            if c == 0:
                m = mc
                p = jnp.exp(s - m)
                l = jnp.sum(p, axis=-1, keepdims=True)
                acc = jnp.dot(p.astype(jnp.bfloat16), vd_ref[grp, keys, :], preferred_element_type=jnp.float32)
            else:
                m_new = jnp.maximum(m, mc)
                alpha = jnp.exp(m - m_new)
                p = jnp.exp(s - m_new)
                l = alpha * l + jnp.sum(p, axis=-1, keepdims=True)
                acc = alpha * acc + jnp.dot(p.astype(jnp.bfloat16), vd_ref[grp, keys, :],
                                            preferred_element_type=jnp.float32)
                m = m_new
        o2 = acc / l
        o_ref[:, pair * LANES:(pair + 1) * LANES] = jnp.where(lo, o2[:tq], o2[tq:])


def gqa(proj, cos_t, sin_t, gq, q_w, k_w, bsz, seq):
    m = proj.shape[0]
    tq = ATT_TQ
    nq = seq // tq
    kvw = ATT_KV_HEADS * ATT_HEAD_DIM
    full = lambda shape: pl.BlockSpec(shape, lambda b, i: (0, 0))
    return pl.pallas_call(
        _gqa_kernel,
        grid=(bsz, nq),
        in_specs=[pl.BlockSpec((tq, ATT_INNER), lambda b, i: (b * nq + i, 1536 // ATT_INNER)),
                  pl.BlockSpec((seq, kvw), lambda b, i: (b, 2048 // kvw)),
                  pl.BlockSpec((seq, kvw), lambda b, i: (b, 2176 // kvw)),
                  pl.BlockSpec((tq, LANES), lambda b, i: (i, 0)),
                  pl.BlockSpec((tq, LANES), lambda b, i: (i, 0)),
                  full((seq, LANES)), full((seq, LANES)),
                  full((ATT_INNER, ATT_INNER)), full((LANES, LANES)),
                  full((1, ATT_INNER)), full((1, LANES))],
        out_specs=pl.BlockSpec((tq, ATT_INNER), lambda b, i: (b * nq + i, 0)),
        out_shape=jax.ShapeDtypeStruct((m, ATT_INNER), jnp.float32),
        scratch_shapes=[pltpu.VMEM((ATT_KV_HEADS, seq, LANES), jnp.bfloat16),
                        pltpu.VMEM((ATT_KV_HEADS, seq, LANES), jnp.bfloat16)],
        compiler_params=pltpu.CompilerParams(
            dimension_semantics=("parallel", "arbitrary"), vmem_limit_bytes=VMEM_LIMIT),
        name="gqa",
    )(proj, proj, proj, cos_t, sin_t, cos_t, sin_t, gq, gq[:LANES, :LANES],
      jnp.tile(q_w, ATT_Q_HEADS).reshape(1, -1), jnp.tile(k_w, ATT_KV_HEADS).reshape(1, -1))


def rope_lane_tables(seq):
    cos, sin = axial_rope_tables(seq)
    cos2 = jnp.repeat(cos, 2, axis=1)
    sin2 = jnp.repeat(sin, 2, axis=1) * jnp.tile(jnp.array([-1.0, 1.0], jnp.float32), ATT_HEAD_DIM // 2)
    return jnp.tile(cos2, (1, 2)), jnp.tile(sin2, (1, 2))


def head_mean_matrix():
    idx = jnp.arange(ATT_INNER) // ATT_HEAD_DIM
    return jnp.where(idx[:, None] == idx[None, :], 1.0 / ATT_HEAD_DIM, 0.0).astype(jnp.bfloat16)


LN_TM = 1024


def _layer_norm(x, g, b):
    mu = jnp.mean(x, axis=-1, keepdims=True)
    xc = x - mu
    var = jnp.mean(xc * xc, axis=-1, keepdims=True)
    return xc * lax.rsqrt(var + LN_EPS) * g + b


def _bf16_split(x):
    hi = x.astype(jnp.bfloat16)
    return hi, (x - hi.astype(jnp.float32)).astype(jnp.bfloat16)


def _mix_ln_router_kernel(ya_ref, yb_ref, x_ref, nw_ref, wa_ref, wb_ref, g_ref, b_ref, rwh_ref, rwl_ref,
                          x1_ref, x1b_ref, aff_ref, *, rms_ya):
    ya = ya_ref[...]
    if rms_ya:
        ya = ya * lax.rsqrt(jnp.mean(ya * ya, axis=-1, keepdims=True) + NORM_EPS) * nw_ref[...]
    mix = (jnp.dot(ya.astype(jnp.bfloat16), wa_ref[...], preferred_element_type=jnp.float32)
           + jnp.dot(yb_ref[...].astype(jnp.bfloat16), wb_ref[...], preferred_element_type=jnp.float32))
    x1 = _layer_norm(DEEPNORM_ALPHA * x_ref[...] + mix, g_ref[...], b_ref[...])
    x1_ref[...] = x1
    x1b_ref[...] = x1.astype(jnp.bfloat16)
    hi, lo = _bf16_split(x1)
    rwh = rwh_ref[...]
    lt = (lax.dot_general(rwh, hi, NT_DIMS, preferred_element_type=jnp.float32)
          + lax.dot_general(rwl_ref[...], hi, NT_DIMS, preferred_element_type=jnp.float32)
          + lax.dot_general(rwh, lo, NT_DIMS, preferred_element_type=jnp.float32))
    e = jnp.exp(lt - jnp.max(lt, axis=0, keepdims=True))
    aff_ref[0] = e / jnp.sum(e, axis=0, keepdims=True)


def mix_ln_router(ya, yb, x, w_out, ln_g, ln_b, router_w, bsz, seq, ya_norm_w=None):
    m, d = x.shape
    tm = LN_TM
    nt = seq // tm
    ka = ya.shape[1]
    rms_ya = ya_norm_w is not None
    nw = (ya_norm_w if rms_ya else jnp.ones((ka,), jnp.float32)).reshape(1, ka)
    wa = w_out[:ka].astype(jnp.bfloat16)
    wb = w_out[ka:].astype(jnp.bfloat16)
    rwh, rwl = _bf16_split(router_w.T)
    ne = router_w.shape[1]
    row = lambda w: pl.BlockSpec((tm, w), lambda i: (i, 0))
    full = lambda a: pl.BlockSpec(a.shape, lambda i: (0, 0))
    g2, b2 = ln_g.reshape(1, d), ln_b.reshape(1, d)
    return pl.pallas_call(
        functools.partial(_mix_ln_router_kernel, rms_ya=rms_ya),
        grid=(m // tm,),
        in_specs=[row(ka), row(yb.shape[1]), row(d), full(nw), full(wa), full(wb), full(g2), full(b2),
                  full(rwh), full(rwl)],
        out_specs=[row(d), row(d), pl.BlockSpec((1, ne, tm), lambda i: (i // nt, 0, i % nt))],
        out_shape=[jax.ShapeDtypeStruct((m, d), jnp.float32),
                   jax.ShapeDtypeStruct((m, d), jnp.bfloat16),
                   jax.ShapeDtypeStruct((bsz, ne, seq), jnp.float32)],
        compiler_params=pltpu.CompilerParams(
            dimension_semantics=("parallel",), vmem_limit_bytes=VMEM_LIMIT),
        name="mix_ln_router",
    )(ya, yb, x, nw, wa, wb, g2, b2, rwh, rwl)


ROUTE_BLK = 256


def _excl_cumsum_lanes(mask, tri):
    n = mask.shape[1]
    blk = tri.shape[0]
    run = jnp.zeros((mask.shape[0], 1), jnp.float32)
    outs = []
    for k in range(n // blk):
        mb = mask[:, k * blk:(k + 1) * blk]
        outs.append(jnp.dot(mb.astype(jnp.bfloat16), tri, preferred_element_type=jnp.float32) + run)
        run = run + jnp.sum(mb, axis=-1, keepdims=True)
    return jnp.concatenate(outs, axis=1)


def _route_kernel(aff_ref, tri_ref, pos_ref, gsel_ref, *, cap):
    a = aff_ref[0]
    capf = jnp.float32(cap)
    bits = jnp.zeros((a.shape[0], 1), jnp.int32)
    for bit in range(30, -1, -1):
        cand = bits | (1 << bit)
        cnt = jnp.sum(jnp.where(a >= pltpu.bitcast(cand, jnp.float32), 1.0, 0.0), axis=-1, keepdims=True)
        bits = jnp.where(cnt >= capf, cand, bits)
    thr = pltpu.bitcast(bits, jnp.float32)
    gt = jnp.where(a > thr, 1.0, 0.0)
    eq = jnp.where(a == thr, 1.0, 0.0)
    need = capf - jnp.sum(gt, axis=-1, keepdims=True)
    tri = tri_ref[...]
    sel = gt + eq * jnp.where(_excl_cumsum_lanes(eq, tri) < need, 1.0, 0.0)
    pos = _excl_cumsum_lanes(sel, tri)
    chosen = sel > 0.5
    pos_ref[0] = jnp.where(chosen, pos, -1.0)
    gsel_ref[0] = jnp.where(chosen, a, 0.0)


def route(aff, cap):
    bsz, ne, seq = aff.shape
    blk = min(ROUTE_BLK, seq)
    idx = jnp.arange(blk)
    tri = (idx[:, None] < idx[None, :]).astype(jnp.bfloat16)
    spec = pl.BlockSpec((1, ne, seq), lambda b: (b, 0, 0))
    return pl.pallas_call(
        functools.partial(_route_kernel, cap=cap),
        grid=(bsz,),
        in_specs=[spec, pl.BlockSpec((blk, blk), lambda b: (0, 0))],
        out_specs=[spec, spec],
        out_shape=[jax.ShapeDtypeStruct((bsz, ne, seq), jnp.float32),
                   jax.ShapeDtypeStruct((bsz, ne, seq), jnp.float32)],
        compiler_params=pltpu.CompilerParams(
            dimension_semantics=("parallel",), vmem_limit_bytes=VMEM_LIMIT),
        name="route",
    )(aff, tri)


GATHER_NE = 4


def _gather_kernel(pos_ref, gsel_ref, x_ref, o_ref, gs_ref):
    nge, cap, d = o_ref.shape
    seq = x_ref.shape[0]
    slot = lax.broadcasted_iota(jnp.int32, (cap, seq), 0).astype(jnp.float32)
    hots = []
    for e in range(nge):
        hit = pos_ref[0, e] == slot
        hots.append(jnp.where(hit, 1.0, 0.0).astype(jnp.bfloat16))
        gs_ref[e] = jnp.sum(jnp.where(hit, gsel_ref[0, e], 0.0), axis=-1, keepdims=True)
    rows = jnp.dot(jnp.concatenate(hots, axis=0), x_ref[...], preferred_element_type=jnp.float32)
    o_ref[...] = rows.astype(jnp.bfloat16).reshape(nge, cap, d)


def gather_tokens(pos, gsel, x1b, cap):
    bsz, ne, seq = pos.shape
    d = x1b.shape[1]
    nge = GATHER_NE
    sel = pl.BlockSpec((1, nge, 1, seq), lambda b, g: (b, g, 0, 0))
    return pl.pallas_call(
        _gather_kernel,
        grid=(bsz, ne // nge),
        in_specs=[sel, sel, pl.BlockSpec((seq, d), lambda b, g: (b, 0))],
        out_specs=[pl.BlockSpec((nge, cap, d), lambda b, g: (g, b, 0)),
                   pl.BlockSpec((nge, cap, 1), lambda b, g: (g, b, 0))],
        out_shape=[jax.ShapeDtypeStruct((ne, bsz * cap, d), jnp.bfloat16),
                   jax.ShapeDtypeStruct((ne, bsz * cap, 1), jnp.float32)],
        compiler_params=pltpu.CompilerParams(
            dimension_semantics=("parallel", "parallel"), vmem_limit_bytes=VMEM_LIMIT),
        name="gather_tokens",
    )(pos.reshape(bsz, ne, 1, seq), gsel.reshape(bsz, ne, 1, seq), x1b)


COMB_TR = 512


def _combine_ln_kernel(ys_ref, post_ref, x1_ref, g_ref, b_ref, o_ref):
    post = post_ref[0]
    ne, cap, d = ys_ref.shape
    lane = lax.broadcasted_iota(jnp.int32, (post.shape[0], cap), 1).astype(jnp.float32)
    hots = [jnp.where(post[:, e:e + 1] == lane, 1.0, 0.0).astype(jnp.bfloat16) for e in range(ne)]
    ffn = jnp.dot(jnp.concatenate(hots, axis=1), ys_ref[...].reshape(ne * cap, d),
                  preferred_element_type=jnp.float32)
    o_ref[...] = _layer_norm(DEEPNORM_ALPHA * x1_ref[...] + ffn, g_ref[...], b_ref[...])


def combine_ln(ys, post, x1, ln_g, ln_b, cap):
    bsz, seq, ne = post.shape
    m, d = x1.shape
    tr = min(COMB_TR, seq)
    nr = seq // tr
    g2, b2 = ln_g.reshape(1, d), ln_b.reshape(1, d)
    vec = pl.BlockSpec((1, d), lambda b, r: (0, 0))
    rows = pl.BlockSpec((tr, d), lambda b, r: (b * nr + r, 0))
    return pl.pallas_call(
        _combine_ln_kernel,
        grid=(bsz, nr),
        in_specs=[pl.BlockSpec((ne, cap, d), lambda b, r: (0, b, 0)),
                  pl.BlockSpec((1, tr, ne), lambda b, r: (b, r, 0)), rows, vec, vec],
        out_specs=rows,
        out_shape=jax.ShapeDtypeStruct((m, d), jnp.float32),
        compiler_params=pltpu.CompilerParams(
            dimension_semantics=("parallel", "parallel"), vmem_limit_bytes=VMEM_LIMIT),
        name="combine_ln",
    )(ys, post, x1, g2, b2)


def moe_block(ya, yb, x2d, w_out, ln1_g, ln1_b, router_w, w1, w3, w2, layer, ln2_g, ln2_b, bsz, seq,
              ya_norm_w=None):
    cap = CAPACITY_FACTOR * seq // N_EXPERTS
    x1, x1b, aff = mix_ln_router(ya, yb, x2d, w_out, ln1_g, ln1_b, router_w, bsz, seq, ya_norm_w)
    pos, gsel = route(aff, cap)
    xs, gs = gather_tokens(pos, gsel, x1b, cap)
    ys = expert_ffn(xs, gs, w1, w3, w2, layer)
    return combine_ln(ys, pos.transpose(0, 2, 1), x1, ln2_g, ln2_b, cap)


SSD_GH = SSD_HEADS // SSD_GROUPS
SSD_GW = SSD_GH * SSD_HEAD_DIM
SSD_UNROLL = 2


def _shift_rows(x, s, row):
    n = x.shape[0]
    if s == 0:
        return x
    y = pltpu.roll(x, (-s) % n, axis=0)
    return jnp.where(row >= -s, y, 0.0) if s < 0 else jnp.where(row < n - s, y, 0.0)


def _conv_silu(x, w_ref, b_ref):
    row = lax.broadcasted_iota(jnp.int32, x.shape, 0)
    pad = (SSD_CONV_K - 1) // 2
    y = b_ref[...] + jnp.zeros_like(x)
    for k in range(SSD_CONV_K):
        y = y + w_ref[k:k + 1, :] * _shift_rows(x, k - pad, row)
    return y * jax.nn.sigmoid(y)


def _cumsum_rows(x, row, reverse):
    n = x.shape[0]
    s = 1
    while s < n:
        if reverse:
            x = x + jnp.where(row + s < n, pltpu.roll(x, n - s, axis=0), 0.0)
        else:
            x = x + jnp.where(row >= s, pltpu.roll(x, s, axis=0), 0.0)
        s *= 2
    return x


def _select_lanes(v, sel):
    hi = v.astype(jnp.bfloat16)
    r1 = v - hi.astype(jnp.float32)
    mid = r1.astype(jnp.bfloat16)
    lo = (r1 - mid.astype(jnp.float32)).astype(jnp.bfloat16)
    pieces = jnp.concatenate([hi, mid, lo], axis=1)
    return jnp.dot(pieces, jnp.concatenate([sel, sel, sel], axis=0), preferred_element_type=jnp.float32)


def _ssd_chunk(ux, ub, uc, dtv, la, st_ref, sele, reverse):
    q = SSD_CHUNK
    bf = jnp.bfloat16
    cols = [SSD_GH + h if reverse else h for h in range(SSD_GH)]
    row = lax.broadcasted_iota(jnp.int32, (q, LANES), 0)
    lane = lax.broadcasted_iota(jnp.int32, (q, LANES), 1)
    lane_lo = lane < SSD_HEAD_DIM
    tri = (row <= lane) if reverse else (row >= lane)
    cum = _cumsum_rows(la, row, reverse)
    cum_t = cum.T
    edge = 0 if reverse else q - 1
    cum_e = _select_lanes(cum, sele)
    tot_e = cum_e[edge:edge + 1, :]
    ub16 = ub.astype(bf)
    uc16 = uc.astype(bf)
    cb = lax.dot_general(uc16, ub16, NT_DIMS, preferred_element_type=jnp.float32)
    xdt = ux * _select_lanes(dtv, sele)
    ys = []
    for p in range(SSD_GH // 2):
        xp = xdt[:, p * LANES:(p + 1) * LANES]
        acc = None
        for half in range(2):
            c = cols[2 * p + half]
            seg = cum[:, c:c + 1] - cum_t[c:c + 1, :]
            lm = jnp.where(tri, jnp.exp(seg), 0.0)
            xh = jnp.where(lane_lo, xp, 0.0) if half == 0 else jnp.where(lane_lo, 0.0, xp)
            t = jnp.dot((cb * lm).astype(bf), xh.astype(bf), preferred_element_type=jnp.float32)
            acc = t if acc is None else acc + t
        ys.append(acc)
    y = jnp.concatenate(ys, axis=1)
    st = st_ref[...]
    y = y + jnp.exp(cum_e) * jnp.dot(uc16, st.astype(bf), preferred_element_type=jnp.float32)
    xe = (xdt * jnp.exp(tot_e - cum_e)).astype(bf)
    new = lax.dot_general(ub16, xe, (((0,), (0,)), ((), ())), preferred_element_type=jnp.float32)
    st_ref[...] = st * jnp.exp(tot_e) + new
    return y


def _ssd_kernel(z_ref, x_ref, b_ref, c_ref, dt_ref, cwx_ref, cwb_ref, cwc_ref, cbx_ref, cbb_ref, cbc_ref,
                dtb_ref, aneg_ref, dsk_ref, sele_ref, o_ref, ux_ref, ub_ref, uc_ref, dtv_ref, la_ref,
                yf_ref, yb_ref, stf_ref, stb_ref):
    ux_ref[...] = _conv_silu(x_ref[...], cwx_ref, cbx_ref)
    ub_ref[...] = _conv_silu(b_ref[...], cwb_ref, cbb_ref)
    uc_ref[...] = _conv_silu(c_ref[...], cwc_ref, cbc_ref)
    t = dt_ref[...] + dtb_ref[...]
    dtv = jnp.maximum(t, 0.0) + jnp.log1p(jnp.exp(-jnp.abs(t)))
    dtv_ref[...] = dtv
    la_ref[...] = dtv * aneg_ref[...]
    stf_ref[...] = jnp.zeros_like(stf_ref)
    stb_ref[...] = jnp.zeros_like(stb_ref)
    q = SSD_CHUNK
    nchunk = x_ref.shape[0] // q

    def body(n, carry):
        for u in range(SSD_UNROLL):
            c = n * SSD_UNROLL + u
            rf = pl.ds(pl.multiple_of(c * q, q), q)
            rb = pl.ds(pl.multiple_of((nchunk - 1 - c) * q, q), q)
            yf_ref[rf, :] = _ssd_chunk(ux_ref[rf, :], ub_ref[rf, :], uc_ref[rf, :], dtv_ref[rf, :], la_ref[rf, :],
                                       stf_ref, sele_ref[0], False)
            yb_ref[rb, :] = _ssd_chunk(ux_ref[rb, :], ub_ref[rb, :], uc_ref[rb, :], dtv_ref[rb, :], la_ref[rb, :],
                                       stb_ref, sele_ref[1], True)
        return carry

    lax.fori_loop(0, nchunk // SSD_UNROLL, body, 0)
    z = z_ref[...]
    y = yf_ref[...] + yb_ref[...] + ux_ref[...] * dsk_ref[...]
    o_ref[...] = y * (z * jax.nn.sigmoid(z))


def ssd(proj, dt, conv_w, conv_b, dt_bias_g, a_neg_g, d_skip_g, bsz, seq):
    m = proj.shape[0]
    gw, n = SSD_GW, SSD_STATE
    f32 = jnp.float32
    kk = SSD_CONV_K
    cb2 = conv_b.reshape(1, -1)
    in_specs = [
        pl.BlockSpec((seq, gw), lambda b, g: (b, g)),
        pl.BlockSpec((seq, gw), lambda b, g: (b, SSD_INNER // gw + g)),
        pl.BlockSpec((seq, n), lambda b, g: (b, 2 * SSD_INNER // n + g)),
        pl.BlockSpec((seq, n), lambda b, g: (b, 2 * SSD_INNER // n + SSD_GROUPS + g)),
        pl.BlockSpec((seq, LANES), lambda b, g: (b, g)),
        pl.BlockSpec((kk, gw), lambda b, g: (0, g)),
        pl.BlockSpec((kk, n), lambda b, g: (0, SSD_INNER // n + g)),
        pl.BlockSpec((kk, n), lambda b, g: (0, SSD_INNER // n + SSD_GROUPS + g)),
        pl.BlockSpec((1, gw), lambda b, g: (0, g)),
        pl.BlockSpec((1, n), lambda b, g: (0, SSD_INNER // n + g)),
        pl.BlockSpec((1, n), lambda b, g: (0, SSD_INNER // n + SSD_GROUPS + g)),
        pl.BlockSpec((1, LANES), lambda b, g: (0, g)),
        pl.BlockSpec((1, LANES), lambda b, g: (0, g)),
        pl.BlockSpec((1, gw), lambda b, g: (0, g)),
        pl.BlockSpec((2, LANES, gw), lambda b, g: (0, 0, 0)),
    ]
    src = jnp.arange(LANES)
    sele = jnp.stack([(src[:, None] == d * SSD_GH + jnp.arange(gw)[None, :] // SSD_HEAD_DIM) for d in range(2)])
    return pl.pallas_call(
        _ssd_kernel,
        grid=(bsz, SSD_GROUPS),
        in_specs=in_specs,
        out_specs=pl.BlockSpec((seq, gw), lambda b, g: (b, g)),
        out_shape=jax.ShapeDtypeStruct((m, SSD_INNER), f32),
        scratch_shapes=[pltpu.VMEM((seq, gw), f32), pltpu.VMEM((seq, n), f32), pltpu.VMEM((seq, n), f32),
                        pltpu.VMEM((seq, LANES), f32), pltpu.VMEM((seq, LANES), f32),
                        pltpu.VMEM((seq, gw), f32), pltpu.VMEM((seq, gw), f32),
                        pltpu.VMEM((n, gw), f32), pltpu.VMEM((n, gw), f32)],
        compiler_params=pltpu.CompilerParams(
            dimension_semantics=("parallel", "parallel"), vmem_limit_bytes=VMEM_LIMIT),
        name="ssd",
    )(proj, proj, proj, proj, dt, conv_w, conv_w, conv_w, cb2, cb2, cb2, dt_bias_g, a_neg_g, d_skip_g,
      sele.astype(jnp.bfloat16))


def _dt_weight(w_dt):
    pad = jnp.zeros((w_dt.shape[0], LANES - 2 * SSD_GH), w_dt.dtype)
    parts = []
    for g in range(SSD_GROUPS):
        parts += [w_dt[:, g * SSD_GH:(g + 1) * SSD_GH],
                  w_dt[:, SSD_HEADS + g * SSD_GH:SSD_HEADS + (g + 1) * SSD_GH], pad]
    return jnp.concatenate(parts, axis=1)


def _group_lanes(v):
    return _dt_weight(v.reshape(1, -1))


def _dt_proj_kernel(x_ref, wh_ref, wl_ref, o_ref):
    hi, lo = _bf16_split(x_ref[...])
    wh = wh_ref[...]
    o_ref[...] = (jnp.dot(hi, wh, preferred_element_type=jnp.float32)
                  + jnp.dot(hi, wl_ref[...], preferred_element_type=jnp.float32)
                  + jnp.dot(lo, wh, preferred_element_type=jnp.float32))


def dt_proj(x, w_dt_g):
    m, k = x.shape
    n = w_dt_g.shape[1]
    wh, wl = _bf16_split(w_dt_g)
    tm = LN_TM
    return pl.pallas_call(
        _dt_proj_kernel,
        grid=(m // tm,),
        in_specs=[pl.BlockSpec((tm, k), lambda i: (i, 0)),
                  pl.BlockSpec((k, n), lambda i: (0, 0)), pl.BlockSpec((k, n), lambda i: (0, 0))],
        out_specs=pl.BlockSpec((tm, n), lambda i: (i, 0)),
        out_shape=jax.ShapeDtypeStruct((m, n), jnp.float32),
        compiler_params=pltpu.CompilerParams(
            dimension_semantics=("parallel",), vmem_limit_bytes=VMEM_LIMIT),
        name="dt_proj",
    )(x, wh, wl)


def _pool_kernel(u_ref, w_ref, sc_ref, o_ref):
    n = u_ref.shape[0]
    row = lax.broadcasted_iota(jnp.int32, (n, POOL_GROUP), 0)
    for gi, win in enumerate(POOL_WINDOWS):
        u = u_ref[:, gi * POOL_GROUP:(gi + 1) * POOL_GROUP]
        half = win // 2
        fwd, bwd = u, u
        s = 1
        while s < half:
            fwd = fwd + _shift_rows(fwd, s, row)
            bwd = bwd + _shift_rows(bwd, -s, row)
            s *= 2
        total = fwd + _shift_rows(bwd, -1, row)
        cnt = (jnp.minimum(row + half, n) - jnp.maximum(row - half, 0)).astype(jnp.float32)
        pooled = total / cnt - u
        mixed = jnp.dot(pooled.astype(jnp.bfloat16), w_ref[gi], preferred_element_type=jnp.float32)
        o_ref[:, gi * POOL_GROUP:(gi + 1) * POOL_GROUP] = mixed * sc_ref[:, gi * POOL_GROUP:(gi + 1) * POOL_GROUP]


def multiscale_pool_kernel(proj, pool_w, pool_scale, bsz, seq):
    m = proj.shape[0]
    return pl.pallas_call(
        _pool_kernel,
        grid=(bsz,),
        in_specs=[pl.BlockSpec((seq, POOL_INNER), lambda b: (b, 0)),
                  pl.BlockSpec(pool_w.shape, lambda b: (0, 0, 0)),
                  pl.BlockSpec((1, POOL_INNER), lambda b: (0, 0))],
        out_specs=pl.BlockSpec((seq, POOL_INNER), lambda b: (b, 0)),
        out_shape=jax.ShapeDtypeStruct((m, POOL_INNER), jnp.float32),
        compiler_params=pltpu.CompilerParams(
            dimension_semantics=("parallel",), vmem_limit_bytes=VMEM_LIMIT),
        name="multiscale_pool",
    )(proj, pool_w.astype(jnp.bfloat16), pool_scale.reshape(1, -1))


def axial_rope_tables(seq):
    rows = seq // GRID_W
    row = jnp.repeat(jnp.arange(rows), GRID_W).astype(jnp.float32)
    col = jnp.tile(jnp.arange(GRID_W), rows).astype(jnp.float32)
    axis_dims = ATT_HEAD_DIM // 2
    freqs = ROPE_THETA ** (-jnp.arange(0, axis_dims, 2, dtype=jnp.float32) / axis_dims)
    ang = jnp.concatenate([row[:, None] * freqs, col[:, None] * freqs], axis=-1)
    return jnp.cos(ang), jnp.sin(ang)


def kernel(x, w_in_ab, ssm_conv_w, ssm_conv_b, ssm_dt_bias, ssm_a_log, ssm_d, ssm_norm_w, attn_q_norm, attn_k_norm, w_out_ab, w_in_cd, pool_w, pool_scale, hgrn_lb_logits, hgrn_norm_w, w_out_cd, router_w, moe_w1, moe_w3, moe_w2, ln1_g, ln1_b, ln2_g, ln2_b):
    bsz, seq, d = x.shape
    m = bsz * seq
    cos_t, sin_t = rope_lane_tables(seq)
    gq = head_mean_matrix()
    lb_all = jnp.cumsum(jax.nn.softmax(hgrn_lb_logits, axis=0), axis=0)
    lb_all = lb_all - lb_all[0]
    x2 = x.reshape(m, d)
    for layer in range(DEPTH):
        j = layer // 2
        if layer % 2 == 0:
            w = w_in_ab[j]
            w_main = jnp.concatenate([w[:, :1536], w[:, 1552:]], axis=1).astype(jnp.bfloat16)
            proj = matmul(x2, w_main)
            dt = dt_proj(x2, _dt_weight(w[:, 1536:1552]))
            ya = ssd(proj, dt, ssm_conv_w[j], ssm_conv_b[j], _group_lanes(ssm_dt_bias[j]),
                     _group_lanes(-jnp.exp(ssm_a_log[j])), jnp.repeat(ssm_d[j], SSD_HEAD_DIM).reshape(1, -1),
                     bsz, seq)
            yb = gqa(proj, cos_t, sin_t, gq, attn_q_norm[j], attn_k_norm[j], bsz, seq)
            w_out, ya_norm_w = w_out_ab[j], ssm_norm_w[j]
        else:
            proj = matmul(x2, w_in_cd[j].astype(jnp.bfloat16))
            ya = multiscale_pool_kernel(proj, pool_w[j], pool_scale[j], bsz, seq)
            yb = hgrn2(proj, lb_all[layer], hgrn_norm_w[j], bsz, seq)
            w_out, ya_norm_w = w_out_cd[j], None
        x2 = moe_block(ya, yb, x2, w_out, ln1_g[layer], ln1_b[layer], router_w[layer],
                       moe_w1, moe_w3, moe_w2, layer, ln2_g[layer], ln2_b[layer], bsz, seq, ya_norm_w)
    return x2.reshape(bsz, seq, d)
```

```python
import functools
import math

import jax
import jax.numpy as jnp
from jax import lax
from jax.experimental import pallas as pl
from jax.experimental.pallas import tpu as pltpu

DEPTH = 4
GRID_W = 64
NORM_EPS = 1e-6
LN_EPS = 1e-5

SSD_HEADS = 8
SSD_HEAD_DIM = 64
SSD_INNER = 512
SSD_GROUPS = 2
SSD_STATE = 128
SSD_CONV_K = 5
SSD_CHUNK = 128

ATT_Q_HEADS = 8
ATT_KV_HEADS = 2
ATT_HEAD_DIM = 64
ATT_INNER = 512
ROPE_THETA = 10000.0

POOL_WINDOWS = (2, 4, 8, 16)
POOL_GROUP = 128
POOL_INNER = 512

HGRN_HEADS = 4
HGRN_HEAD_DIM = 128
HGRN_INNER = 512

N_EXPERTS = 16
CAPACITY_FACTOR = 2

DEEPNORM_ALPHA = (2 * DEPTH) ** 0.25

LANES = 128
VMEM_LIMIT = 56 * 1024 * 1024
NT_DIMS = (((1,), (1,)), ((), ()))


def _mm_kernel(x_ref, w_ref, o_ref):
    o_ref[...] = jnp.dot(x_ref[...].astype(jnp.bfloat16), w_ref[...],
                         preferred_element_type=jnp.float32)


MM_TM = 1024


def matmul(x, w_bf16, *, tm=MM_TM):
    m, k = x.shape
    n = w_bf16.shape[1]
    return pl.pallas_call(
        _mm_kernel,
        grid=(m // tm,),
        in_specs=[pl.BlockSpec((tm, k), lambda i: (i, 0)),
                  pl.BlockSpec((k, n), lambda i: (0, 0))],
        out_specs=pl.BlockSpec((tm, n), lambda i: (i, 0)),
        out_shape=jax.ShapeDtypeStruct((m, n), jnp.float32),
        compiler_params=pltpu.CompilerParams(
            dimension_semantics=("parallel",), vmem_limit_bytes=VMEM_LIMIT),
        name="matmul",
    )(x, w_bf16)


FFN_TF = 512


FFN_ROWS = 512


def _ffn_kernel(xs_ref, gs_ref, w1_ref, w3_ref, w2_ref, o_ref, acc_ref):
    f = pl.program_id(1)

    @pl.when(f == 0)
    def _():
        acc_ref[...] = jnp.zeros_like(acc_ref)

    w1 = w1_ref[...].astype(jnp.bfloat16)
    w3 = w3_ref[...].astype(jnp.bfloat16)
    w2 = w2_ref[...].astype(jnp.bfloat16)
    rc = min(FFN_ROWS, xs_ref.shape[0])
    for c in range(xs_ref.shape[0] // rc):
        rows = slice(c * rc, (c + 1) * rc)
        xc = xs_ref[rows, :]
        h1 = jnp.dot(xc, w1, preferred_element_type=jnp.float32)
        h3 = jnp.dot(xc, w3, preferred_element_type=jnp.float32)
        hdn = (h1 * jax.nn.sigmoid(h1) * h3).astype(jnp.bfloat16)
        acc_ref[rows, :] += jnp.dot(hdn, w2, preferred_element_type=jnp.float32)

    @pl.when(f == pl.num_programs(1) - 1)
    def _():
        o_ref[...] = (acc_ref[...] * gs_ref[...]).astype(o_ref.dtype)


def expert_ffn(xs, gs, w1, w3, w2, layer):
    e, r, d = xs.shape
    ff = w1.shape[3]
    tf = FFN_TF
    return pl.pallas_call(
        _ffn_kernel,
        grid=(e, ff // tf),
        in_specs=[pl.BlockSpec((None, r, d), lambda i, j: (i, 0, 0)),
                  pl.BlockSpec((None, r, 1), lambda i, j: (i, 0, 0)),
                  pl.BlockSpec((None, None, d, tf), lambda i, j: (layer, i, 0, j)),
                  pl.BlockSpec((None, None, d, tf), lambda i, j: (layer, i, 0, j)),
                  pl.BlockSpec((None, None, tf, d), lambda i, j: (layer, i, j, 0))],
        out_specs=pl.BlockSpec((None, r, d), lambda i, j: (i, 0, 0)),
        out_shape=jax.ShapeDtypeStruct((e, r, d), jnp.bfloat16),
        scratch_shapes=[pltpu.VMEM((r, d), jnp.float32)],
        compiler_params=pltpu.CompilerParams(
            dimension_semantics=("parallel", "arbitrary"), vmem_limit_bytes=VMEM_LIMIT),
        name="expert_ffn",
    )(xs, gs, w1, w3, w2)


HG_T = 128
HG_UNROLL = 2


def _hgrn_gate(raw, log_lb, log1m_lb, one_m_lb):
    e = jnp.exp(-jnp.abs(raw))
    den = 1.0 + e
    ls = jnp.minimum(raw, 0.0) - jnp.log(den)
    bb = log1m_lb + ls
    mx = jnp.maximum(log_lb, bb)
    logf = mx + jnp.log(1.0 + jnp.exp(-jnp.abs(log_lb - bb)))
    sig_neg = jnp.where(raw >= 0.0, e, 1.0) / den
    return logf * math.log2(math.e), one_m_lb * sig_neg


HG_S = 8


def _hgrn_block(q, k, lf, v, st_ref, backward):
    t, s8 = HG_T, HG_S
    hd = HGRN_HEAD_DIM
    bf = jnp.bfloat16
    row8 = lax.broadcasted_iota(jnp.int32, (s8, hd), 0)
    r8 = lax.broadcasted_iota(jnp.int32, (s8, t), 0)
    lt = lax.broadcasted_iota(jnp.int32, (s8, t), 1)
    sdiag, cloc = [], []
    for i in range(t // s8):
        rows = slice(i * s8, (i + 1) * s8)
        qh, kh_, lfh = q[rows], k[rows], lf[rows]
        w = jnp.zeros_like(qh)
        sc = jnp.zeros((s8, t), jnp.float32)
        for d in range(s8):
            sh = (s8 - d) % s8 if backward else d
            kd = kh_ if d == 0 else pltpu.roll(kh_, sh, axis=0)
            lfd = lfh if d == 0 else pltpu.roll(lfh, sh, axis=0)
            p = qh * kd * jnp.exp2(w)
            valid = (r8 + d <= s8 - 1) if backward else (r8 >= d)
            col = jnp.where(valid, (r8 + d if backward else r8 - d) + i * s8, -1)
            sc = jnp.where(lt == col, jnp.sum(p, axis=-1, keepdims=True), sc)
            w = w + lfd
        sdiag.append(sc)
        c = lfh
        s = 1
        while s < s8:
            if backward:
                c = c + jnp.where(row8 + s <= s8 - 1, pltpu.roll(c, s8 - s, axis=0), 0.0)
            else:
                c = c + jnp.where(row8 >= s, pltpu.roll(c, s, axis=0), 0.0)
            s *= 2
        cloc.append(c)
    scores = jnp.concatenate(sdiag, axis=0)
    cl = jnp.concatenate(cloc, axis=0)
    row = lax.broadcasted_iota(jnp.int32, (t, hd), 0)
    ti = lax.broadcasted_iota(jnp.int32, (t, t), 0)
    tj = lax.broadcasted_iota(jnp.int32, (t, t), 1)
    h = s8
    while h < t:
        is_far = ((row // h) % 2 == 0) if backward else ((row // h) % 2 == 1)
        parts = []
        for blk in range(t // (2 * h)):
            e = blk * 2 * h + (h if backward else h - 1)
            parts.append(jnp.broadcast_to(cl[e:e + 1, :], (2 * h, hd)))
        tn = jnp.concatenate(parts, axis=0) if len(parts) > 1 else parts[0]
        qt = jnp.where(is_far, q * jnp.exp2(cl), 0.0).astype(bf)
        kt = jnp.where(is_far, 0.0, k * jnp.exp2(jnp.where(is_far, 0.0, tn - cl))).astype(bf)
        s_h = lax.dot_general(qt, kt, NT_DIMS, preferred_element_type=jnp.float32)
        if 2 * h < t:
            s_h = jnp.where(ti // (2 * h) == tj // (2 * h), s_h, 0.0)
        scores = scores + s_h
        cl = cl + jnp.where(is_far, tn, 0.0)
        h *= 2
    edge = 0 if backward else t - 1
    tot = cl[edge:edge + 1, :]
    st = st_ref[...]
    vb = v.astype(bf)
    o = jnp.dot(scores.astype(bf), vb, preferred_element_type=jnp.float32)
    o = o + lax.dot_general((q * jnp.exp2(cl)).astype(bf), st.astype(bf), NT_DIMS,
                            preferred_element_type=jnp.float32)
    kh = (k * jnp.exp2(tot - cl)).astype(bf)
    ut = lax.dot_general(vb, kh, (((0,), (0,)), ((), ())), preferred_element_type=jnp.float32)
    st_ref[...] = st * jnp.exp2(tot) + ut
    return o


def _hgrn_kernel(q_ref, ff_ref, fb_ref, i_ref, g_ref, lb_ref, nw_ref, o_ref,
                 lff_ref, kf_ref, lfb_ref, kb_ref, of_ref, ob_ref, stf_ref, stb_ref):
    lb = lb_ref[...]
    log_lb = jnp.log(lb)
    log1m_lb = jnp.log1p(-lb)
    one_m_lb = 1.0 - lb
    lf, kk = _hgrn_gate(ff_ref[...], log_lb, log1m_lb, one_m_lb)
    lff_ref[...] = lf
    kf_ref[...] = kk
    lf, kk = _hgrn_gate(fb_ref[...], log_lb, log1m_lb, one_m_lb)
    lfb_ref[...] = lf
    kb_ref[...] = kk
    stf_ref[...] = jnp.zeros_like(stf_ref)
    stb_ref[...] = jnp.zeros_like(stb_ref)
    nblk = q_ref.shape[0] // HG_T

    def body(n, carry):
        for u in range(HG_UNROLL):
            blk = n * HG_UNROLL + u
            rf = pl.ds(pl.multiple_of(blk * HG_T, HG_T), HG_T)
            rb = pl.ds(pl.multiple_of((nblk - 1 - blk) * HG_T, HG_T), HG_T)
            of_ref[rf, :] = _hgrn_block(q_ref[rf, :], kf_ref[rf, :], lff_ref[rf, :], i_ref[rf, :], stf_ref, False)
            ob_ref[rb, :] = _hgrn_block(q_ref[rb, :], kb_ref[rb, :], lfb_ref[rb, :], i_ref[rb, :], stb_ref, True)
        return carry

    lax.fori_loop(0, nblk // HG_UNROLL, body, 0)
    o = of_ref[...] + ob_ref[...]
    o = o * lax.rsqrt(jnp.mean(o * o, axis=-1, keepdims=True) + NORM_EPS) * nw_ref[...]
    o_ref[...] = o * jax.nn.sigmoid(g_ref[...])


def hgrn2(proj, lb, norm_w, bsz, seq):
    m = proj.shape[0]
    hd = HGRN_HEAD_DIM
    nh = HGRN_HEADS

    def col(base):
        return pl.BlockSpec((seq, hd), lambda b, h: (b, base * nh + h))

    vec = pl.BlockSpec((1, hd), lambda b, h: (0, h))
    f32 = jnp.float32
    return pl.pallas_call(
        _hgrn_kernel,
        grid=(bsz, nh),
        in_specs=[col(1), col(2), col(3), col(4), col(5), vec, vec],
        out_specs=pl.BlockSpec((seq, hd), lambda b, h: (b, h)),
        out_shape=jax.ShapeDtypeStruct((m, HGRN_INNER), f32),
        scratch_shapes=[pltpu.VMEM((seq, hd), f32) for _ in range(6)]
        + [pltpu.VMEM((hd, hd), f32) for _ in range(2)],
        compiler_params=pltpu.CompilerParams(
            dimension_semantics=("parallel", "parallel"), vmem_limit_bytes=VMEM_LIMIT),
        name="hgrn2",
    )(proj, proj, proj, proj, proj, lb.reshape(1, -1), norm_w.reshape(1, -1))


ATT_TQ = 512
ATT_KC = 1024


def _group_mean_sq(x, g_ref):
    x2 = x * x
    hi = x2.astype(jnp.bfloat16)
    lo = (x2 - hi.astype(jnp.float32)).astype(jnp.bfloat16)
    g = g_ref[...]
    return (jnp.dot(hi, g, preferred_element_type=jnp.float32)
            + jnp.dot(lo, g, preferred_element_type=jnp.float32))


def _norm_rope(x, g_ref, w, cos, sin_signed):
    n = x.shape[1]
    y = x * lax.rsqrt(_group_mean_sq(x, g_ref) + NORM_EPS) * w
    lane = lax.broadcasted_iota(jnp.int32, y.shape, 1)
    partner = jnp.where(lane % 2 == 0, pltpu.roll(y, n - 1, axis=1), pltpu.roll(y, 1, axis=1))
    return y * cos + partner * sin_signed


def _dup_halves(x):
    lane = lax.broadcasted_iota(jnp.int32, x.shape, 1)
    sw = pltpu.roll(x, ATT_HEAD_DIM, axis=1)
    lo = lane < ATT_HEAD_DIM
    return jnp.where(lo, x, sw), jnp.where(lo, sw, x)


def _gqa_kernel(q_ref, k_ref, v_ref, cq_ref, sq_ref, ck_ref, sk_ref, gq_ref, gk_ref, qw_ref, kw_ref,
                o_ref, kd_ref, vd_ref):
    @pl.when(pl.program_id(1) == 0)
    def _():
        kr = _norm_rope(k_ref[...], gk_ref, kw_ref[...], ck_ref[...], sk_ref[...])
        k0, k1 = _dup_halves(kr)
        kd_ref[0] = k0.astype(jnp.bfloat16)
        kd_ref[1] = k1.astype(jnp.bfloat16)
        v0, v1 = _dup_halves(v_ref[...])
        vd_ref[0] = v0.astype(jnp.bfloat16)
        vd_ref[1] = v1.astype(jnp.bfloat16)

    tq = q_ref.shape[0]
    reps = ATT_INNER // LANES
    cos = jnp.concatenate([cq_ref[...]] * reps, axis=1)
    sin = jnp.concatenate([sq_ref[...]] * reps, axis=1)
    qr = _norm_rope(q_ref[...], gq_ref, qw_ref[...], cos, sin) * (ATT_HEAD_DIM ** -0.5)
    lane = lax.broadcasted_iota(jnp.int32, (tq, LANES), 1)
    lo = lane < ATT_HEAD_DIM
    rep = ATT_Q_HEADS // ATT_KV_HEADS
    for pair in range(ATT_Q_HEADS // 2):
        grp = (2 * pair) // rep
        qp = qr[:, pair * LANES:(pair + 1) * LANES]
        q2 = jnp.concatenate([jnp.where(lo, qp, 0.0), jnp.where(lo, 0.0, qp)], axis=0).astype(jnp.bfloat16)
        m = l = acc = None
        kc = min(ATT_KC, kd_ref.shape[1])
        for c in range(kd_ref.shape[1] // kc):
            keys = slice(c * kc, (c + 1) * kc)
            s = lax.dot_general(q2, kd_ref[grp, keys, :], NT_DIMS, preferred_element_type=jnp.float32)
            mc = jnp.max(s, axis=-1, keepdims=True)
            if c == 0:
                m = mc
                p = jnp.exp(s - m)
                l = jnp.sum(p, axis=-1, keepdims=True)
                acc = jnp.dot(p.astype(jnp.bfloat16), vd_ref[grp, keys, :], preferred_element_type=jnp.float32)
            else:
                m_new = jnp.maximum(m, mc)
                alpha = jnp.exp(m - m_new)
                p = jnp.exp(s - m_new)
                l = alpha * l + jnp.sum(p, axis=-1, keepdims=True)
                acc = alpha * acc + jnp.dot(p.astype(jnp.bfloat16), vd_ref[grp, keys, :],
                                            preferred_element_type=jnp.float32)
                m = m_new
        o2 = acc / l
        o_ref[:, pair * LANES:(pair + 1) * LANES] = jnp.where(lo, o2[:tq], o2[tq:])


def gqa(proj, cos_t, sin_t, gq, q_w, k_w, bsz, seq):
    m = proj.shape[0]
    tq = ATT_TQ
    nq = seq // tq
    kvw = ATT_KV_HEADS * ATT_HEAD_DIM
    full = lambda shape: pl.BlockSpec(shape, lambda b, i: (0, 0))
    return pl.pallas_call(
        _gqa_kernel,
        grid=(bsz, nq),
        in_specs=[pl.BlockSpec((tq, ATT_INNER), lambda b, i: (b * nq + i, 1536 // ATT_INNER)),
                  pl.BlockSpec((seq, kvw), lambda b, i: (b, 2048 // kvw)),
                  pl.BlockSpec((seq, kvw), lambda b, i: (b, 2176 // kvw)),
                  pl.BlockSpec((tq, LANES), lambda b, i: (i, 0)),
                  pl.BlockSpec((tq, LANES), lambda b, i: (i, 0)),
                  full((seq, LANES)), full((seq, LANES)),
                  full((ATT_INNER, ATT_INNER)), full((LANES, LANES)),
                  full((1, ATT_INNER)), full((1, LANES))],
        out_specs=pl.BlockSpec((tq, ATT_INNER), lambda b, i: (b * nq + i, 0)),
        out_shape=jax.ShapeDtypeStruct((m, ATT_INNER), jnp.float32),
        scratch_shapes=[pltpu.VMEM((ATT_KV_HEADS, seq, LANES), jnp.bfloat16),
                        pltpu.VMEM((ATT_KV_HEADS, seq, LANES), jnp.bfloat16)],
        compiler_params=pltpu.CompilerParams(
            dimension_semantics=("parallel", "arbitrary"), vmem_limit_bytes=VMEM_LIMIT),
        name="gqa",
    )(proj, proj, proj, cos_t, sin_t, cos_t, sin_t, gq, gq[:LANES, :LANES],
      jnp.tile(q_w, ATT_Q_HEADS).reshape(1, -1), jnp.tile(k_w, ATT_KV_HEADS).reshape(1, -1))


def rope_lane_tables(seq):
    cos, sin = axial_rope_tables(seq)
    cos2 = jnp.repeat(cos, 2, axis=1)
    sin2 = jnp.repeat(sin, 2, axis=1) * jnp.tile(jnp.array([-1.0, 1.0], jnp.float32), ATT_HEAD_DIM // 2)
    return jnp.tile(cos2, (1, 2)), jnp.tile(sin2, (1, 2))


def head_mean_matrix():
    idx = jnp.arange(ATT_INNER) // ATT_HEAD_DIM
    return jnp.where(idx[:, None] == idx[None, :], 1.0 / ATT_HEAD_DIM, 0.0).astype(jnp.bfloat16)


LN_TM = 1024


def _layer_norm(x, g, b):
    mu = jnp.mean(x, axis=-1, keepdims=True)
    xc = x - mu
    var = jnp.mean(xc * xc, axis=-1, keepdims=True)
    return xc * lax.rsqrt(var + LN_EPS) * g + b


def _bf16_split(x):
    hi = x.astype(jnp.bfloat16)
    return hi, (x - hi.astype(jnp.float32)).astype(jnp.bfloat16)


def _mix_ln_router_kernel(ya_ref, yb_ref, x_ref, nw_ref, wa_ref, wb_ref, g_ref, b_ref, rwh_ref, rwl_ref,
                          x1_ref, x1b_ref, aff_ref, *, rms_ya):
    ya = ya_ref[...]
    if rms_ya:
        ya = ya * lax.rsqrt(jnp.mean(ya * ya, axis=-1, keepdims=True) + NORM_EPS) * nw_ref[...]
    mix = (jnp.dot(ya.astype(jnp.bfloat16), wa_ref[...], preferred_element_type=jnp.float32)
           + jnp.dot(yb_ref[...].astype(jnp.bfloat16), wb_ref[...], preferred_element_type=jnp.float32))
    x1 = _layer_norm(DEEPNORM_ALPHA * x_ref[...] + mix, g_ref[...], b_ref[...])
    x1_ref[...] = x1
    x1b_ref[...] = x1.astype(jnp.bfloat16)
    hi, lo = _bf16_split(x1)
    rwh = rwh_ref[...]
    lt = (lax.dot_general(rwh, hi, NT_DIMS, preferred_element_type=jnp.float32)
          + lax.dot_general(rwl_ref[...], hi, NT_DIMS, preferred_element_type=jnp.float32)
          + lax.dot_general(rwh, lo, NT_DIMS, preferred_element_type=jnp.float32))
    e = jnp.exp(lt - jnp.max(lt, axis=0, keepdims=True))
    aff_ref[0] = e / jnp.sum(e, axis=0, keepdims=True)


def mix_ln_router(ya, yb, x, w_out, ln_g, ln_b, router_w, bsz, seq, ya_norm_w=None):
    m, d = x.shape
    tm = LN_TM
    nt = seq // tm
    ka = ya.shape[1]
    rms_ya = ya_norm_w is not None
    nw = (ya_norm_w if rms_ya else jnp.ones((ka,), jnp.float32)).reshape(1, ka)
    wa = w_out[:ka].astype(jnp.bfloat16)
    wb = w_out[ka:].astype(jnp.bfloat16)
    rwh, rwl = _bf16_split(router_w.T)
    ne = router_w.shape[1]
    row = lambda w: pl.BlockSpec((tm, w), lambda i: (i, 0))
    full = lambda a: pl.BlockSpec(a.shape, lambda i: (0, 0))
    g2, b2 = ln_g.reshape(1, d), ln_b.reshape(1, d)
    return pl.pallas_call(
        functools.partial(_mix_ln_router_kernel, rms_ya=rms_ya),
        grid=(m // tm,),
        in_specs=[row(ka), row(yb.shape[1]), row(d), full(nw), full(wa), full(wb), full(g2), full(b2),
                  full(rwh), full(rwl)],
        out_specs=[row(d), row(d), pl.BlockSpec((1, ne, tm), lambda i: (i // nt, 0, i % nt))],
        out_shape=[jax.ShapeDtypeStruct((m, d), jnp.float32),
                   jax.ShapeDtypeStruct((m, d), jnp.bfloat16),
                   jax.ShapeDtypeStruct((bsz, ne, seq), jnp.float32)],
        compiler_params=pltpu.CompilerParams(
            dimension_semantics=("parallel",), vmem_limit_bytes=VMEM_LIMIT),
        name="mix_ln_router",
    )(ya, yb, x, nw, wa, wb, g2, b2, rwh, rwl)


ROUTE_BLK = 256


def _excl_cumsum_lanes(mask, tri):
    n = mask.shape[1]
    blk = tri.shape[0]
    run = jnp.zeros((mask.shape[0], 1), jnp.float32)
    outs = []
    for k in range(n // blk):
        mb = mask[:, k * blk:(k + 1) * blk]
        outs.append(jnp.dot(mb.astype(jnp.bfloat16), tri, preferred_element_type=jnp.float32) + run)
        run = run + jnp.sum(mb, axis=-1, keepdims=True)
    return jnp.concatenate(outs, axis=1)


def _route_kernel(aff_ref, tri_ref, pos_ref, gsel_ref, *, cap):
    a = aff_ref[0]
    capf = jnp.float32(cap)
    bits = jnp.zeros((a.shape[0], 1), jnp.int32)
    for bit in range(30, -1, -1):
        cand = bits | (1 << bit)
        cnt = jnp.sum(jnp.where(a >= pltpu.bitcast(cand, jnp.float32), 1.0, 0.0), axis=-1, keepdims=True)
        bits = jnp.where(cnt >= capf, cand, bits)
    thr = pltpu.bitcast(bits, jnp.float32)
    gt = jnp.where(a > thr, 1.0, 0.0)
    eq = jnp.where(a == thr, 1.0, 0.0)
    need = capf - jnp.sum(gt, axis=-1, keepdims=True)
    tri = tri_ref[...]
    sel = gt + eq * jnp.where(_excl_cumsum_lanes(eq, tri) < need, 1.0, 0.0)
    pos = _excl_cumsum_lanes(sel, tri)
    chosen = sel > 0.5
    pos_ref[0] = jnp.where(chosen, pos, -1.0)
    gsel_ref[0] = jnp.where(chosen, a, 0.0)


def route(aff, cap):
    bsz, ne, seq = aff.shape
    blk = min(ROUTE_BLK, seq)
    idx = jnp.arange(blk)
    tri = (idx[:, None] < idx[None, :]).astype(jnp.bfloat16)
    spec = pl.BlockSpec((1, ne, seq), lambda b: (b, 0, 0))
    return pl.pallas_call(
        functools.partial(_route_kernel, cap=cap),
        grid=(bsz,),
        in_specs=[spec, pl.BlockSpec((blk, blk), lambda b: (0, 0))],
        out_specs=[spec, spec],
        out_shape=[jax.ShapeDtypeStruct((bsz, ne, seq), jnp.float32),
                   jax.ShapeDtypeStruct((bsz, ne, seq), jnp.float32)],
        compiler_params=pltpu.CompilerParams(
            dimension_semantics=("parallel",), vmem_limit_bytes=VMEM_LIMIT),
        name="route",
    )(aff, tri)


GATHER_NE = 4


def _gather_kernel(pos_ref, gsel_ref, x_ref, o_ref, gs_ref):
    nge, cap, d = o_ref.shape
    seq = x_ref.shape[0]
    slot = lax.broadcasted_iota(jnp.int32, (cap, seq), 0).astype(jnp.float32)
    hots = []
    for e in range(nge):
        hit = pos_ref[0, e] == slot
        hots.append(jnp.where(hit, 1.0, 0.0).astype(jnp.bfloat16))
        gs_ref[e] = jnp.sum(jnp.where(hit, gsel_ref[0, e], 0.0), axis=-1, keepdims=True)
    rows = jnp.dot(jnp.concatenate(hots, axis=0), x_ref[...], preferred_element_type=jnp.float32)
    o_ref[...] = rows.astype(jnp.bfloat16).reshape(nge, cap, d)


def gather_tokens(pos, gsel, x1b, cap):
    bsz, ne, seq = pos.shape
    d = x1b.shape[1]
    nge = GATHER_NE
    sel = pl.BlockSpec((1, nge, 1, seq), lambda b, g: (b, g, 0, 0))
    return pl.pallas_call(
        _gather_kernel,
        grid=(bsz, ne // nge),
        in_specs=[sel, sel, pl.BlockSpec((seq, d), lambda b, g: (b, 0))],
        out_specs=[pl.BlockSpec((nge, cap, d), lambda b, g: (g, b, 0)),
                   pl.BlockSpec((nge, cap, 1), lambda b, g: (g, b, 0))],
        out_shape=[jax.ShapeDtypeStruct((ne, bsz * cap, d), jnp.bfloat16),
                   jax.ShapeDtypeStruct((ne, bsz * cap, 1), jnp.float32)],
        compiler_params=pltpu.CompilerParams(
            dimension_semantics=("parallel", "parallel"), vmem_limit_bytes=VMEM_LIMIT),
        name="gather_tokens",
    )(pos.reshape(bsz, ne, 1, seq), gsel.reshape(bsz, ne, 1, seq), x1b)


COMB_TR = 512


def _combine_ln_kernel(ys_ref, post_ref, x1_ref, g_ref, b_ref, o_ref):
    post = post_ref[0]
    ne, cap, d = ys_ref.shape
    lane = lax.broadcasted_iota(jnp.int32, (post.shape[0], cap), 1).astype(jnp.float32)
    hots = [jnp.where(post[:, e:e + 1] == lane, 1.0, 0.0).astype(jnp.bfloat16) for e in range(ne)]
    ffn = jnp.dot(jnp.concatenate(hots, axis=1), ys_ref[...].reshape(ne * cap, d),
                  preferred_element_type=jnp.float32)
    o_ref[...] = _layer_norm(DEEPNORM_ALPHA * x1_ref[...] + ffn, g_ref[...], b_ref[...])


def combine_ln(ys, post, x1, ln_g, ln_b, cap):
    bsz, seq, ne = post.shape
    m, d = x1.shape
    tr = min(COMB_TR, seq)
    nr = seq // tr
    g2, b2 = ln_g.reshape(1, d), ln_b.reshape(1, d)
    vec = pl.BlockSpec((1, d), lambda b, r: (0, 0))
    rows = pl.BlockSpec((tr, d), lambda b, r: (b * nr + r, 0))
    return pl.pallas_call(
        _combine_ln_kernel,
        grid=(bsz, nr),
        in_specs=[pl.BlockSpec((ne, cap, d), lambda b, r: (0, b, 0)),
                  pl.BlockSpec((1, tr, ne), lambda b, r: (b, r, 0)), rows, vec, vec],
        out_specs=rows,
        out_shape=jax.ShapeDtypeStruct((m, d), jnp.float32),
        compiler_params=pltpu.CompilerParams(
            dimension_semantics=("parallel", "parallel"), vmem_limit_bytes=VMEM_LIMIT),
        name="combine_ln",
    )(ys, post, x1, g2, b2)


def moe_block(ya, yb, x2d, w_out, ln1_g, ln1_b, router_w, w1, w3, w2, layer, ln2_g, ln2_b, bsz, seq,
              ya_norm_w=None):
    cap = CAPACITY_FACTOR * seq // N_EXPERTS
    x1, x1b, aff = mix_ln_router(ya, yb, x2d, w_out, ln1_g, ln1_b, router_w, bsz, seq, ya_norm_w)
    pos, gsel = route(aff, cap)
    xs, gs = gather_tokens(pos, gsel, x1b, cap)
    ys = expert_ffn(xs, gs, w1, w3, w2, layer)
    return combine_ln(ys, pos.transpose(0, 2, 1), x1, ln2_g, ln2_b, cap)


SSD_GH = SSD_HEADS // SSD_GROUPS
SSD_GW = SSD_GH * SSD_HEAD_DIM
SSD_UNROLL = 2


def _shift_rows(x, s, row):
    n = x.shape[0]
    if s == 0:
        return x
    y = pltpu.roll(x, (-s) % n, axis=0)
    return jnp.where(row >= -s, y, 0.0) if s < 0 else jnp.where(row < n - s, y, 0.0)


def _conv_silu(x, w_ref, b_ref):
    row = lax.broadcasted_iota(jnp.int32, x.shape, 0)
    pad = (SSD_CONV_K - 1) // 2
    y = b_ref[...] + jnp.zeros_like(x)
    for k in range(SSD_CONV_K):
        y = y + w_ref[k:k + 1, :] * _shift_rows(x, k - pad, row)
    return y * jax.nn.sigmoid(y)


def _cumsum_rows(x, row, reverse):
    n = x.shape[0]
    s = 1
    while s < n:
        if reverse:
            x = x + jnp.where(row + s < n, pltpu.roll(x, n - s, axis=0), 0.0)
        else:
            x = x + jnp.where(row >= s, pltpu.roll(x, s, axis=0), 0.0)
        s *= 2
    return x


def _select_lanes(v, sel):
    hi = v.astype(jnp.bfloat16)
    r1 = v - hi.astype(jnp.float32)
    mid = r1.astype(jnp.bfloat16)
    lo = (r1 - mid.astype(jnp.float32)).astype(jnp.bfloat16)
    pieces = jnp.concatenate([hi, mid, lo], axis=1)
    return jnp.dot(pieces, jnp.concatenate([sel, sel, sel], axis=0), preferred_element_type=jnp.float32)


def _ssd_chunk(ux, ub, uc, dtv, la, st_ref, sele, reverse):
    q = SSD_CHUNK
    bf = jnp.bfloat16
    cols = [SSD_GH + h if reverse else h for h in range(SSD_GH)]
    row = lax.broadcasted_iota(jnp.int32, (q, LANES), 0)
    lane = lax.broadcasted_iota(jnp.int32, (q, LANES), 1)
    lane_lo = lane < SSD_HEAD_DIM
    tri = (row <= lane) if reverse else (row >= lane)
    cum = _cumsum_rows(la, row, reverse)
    cum_t = cum.T
    edge = 0 if reverse else q - 1
    cum_e = _select_lanes(cum, sele)
    tot_e = cum_e[edge:edge + 1, :]
    ub16 = ub.astype(bf)
    uc16 = uc.astype(bf)
    cb = lax.dot_general(uc16, ub16, NT_DIMS, preferred_element_type=jnp.float32)
    xdt = ux * _select_lanes(dtv, sele)
    ys = []
    for p in range(SSD_GH // 2):
        xp = xdt[:, p * LANES:(p + 1) * LANES]
        acc = None
        for half in range(2):
            c = cols[2 * p + half]
            seg = cum[:, c:c + 1] - cum_t[c:c + 1, :]
            lm = jnp.where(tri, jnp.exp(seg), 0.0)
            xh = jnp.where(lane_lo, xp, 0.0) if half == 0 else jnp.where(lane_lo, 0.0, xp)
            t = jnp.dot((cb * lm).astype(bf), xh.astype(bf), preferred_element_type=jnp.float32)
            acc = t if acc is None else acc + t
        ys.append(acc)
    y = jnp.concatenate(ys, axis=1)
    st = st_ref[...]
    y = y + jnp.exp(cum_e) * jnp.dot(uc16, st.astype(bf), preferred_element_type=jnp.float32)
    xe = (xdt * jnp.exp(tot_e - cum_e)).astype(bf)
    new = lax.dot_general(ub16, xe, (((0,), (0,)), ((), ())), preferred_element_type=jnp.float32)
    st_ref[...] = st * jnp.exp(tot_e) + new
    return y


def _ssd_kernel(z_ref, x_ref, b_ref, c_ref, dt_ref, cwx_ref, cwb_ref, cwc_ref, cbx_ref, cbb_ref, cbc_ref,
                dtb_ref, aneg_ref, dsk_ref, sele_ref, o_ref, ux_ref, ub_ref, uc_ref, dtv_ref, la_ref,
                yf_ref, yb_ref, stf_ref, stb_ref):
    ux_ref[...] = _conv_silu(x_ref[...], cwx_ref, cbx_ref)
    ub_ref[...] = _conv_silu(b_ref[...], cwb_ref, cbb_ref)
    uc_ref[...] = _conv_silu(c_ref[...], cwc_ref, cbc_ref)
    t = dt_ref[...] + dtb_ref[...]
    dtv = jnp.maximum(t, 0.0) + jnp.log1p(jnp.exp(-jnp.abs(t)))
    dtv_ref[...] = dtv
    la_ref[...] = dtv * aneg_ref[...]
    stf_ref[...] = jnp.zeros_like(stf_ref)
    stb_ref[...] = jnp.zeros_like(stb_ref)
    q = SSD_CHUNK
    nchunk = x_ref.shape[0] // q

    def body(n, carry):
        for u in range(SSD_UNROLL):
            c = n * SSD_UNROLL + u
            rf = pl.ds(pl.multiple_of(c * q, q), q)
            rb = pl.ds(pl.multiple_of((nchunk - 1 - c) * q, q), q)
            yf_ref[rf, :] = _ssd_chunk(ux_ref[rf, :], ub_ref[rf, :], uc_ref[rf, :], dtv_ref[rf, :], la_ref[rf, :],
                                       stf_ref, sele_ref[0], False)
            yb_ref[rb, :] = _ssd_chunk(ux_ref[rb, :], ub_ref[rb, :], uc_ref[rb, :], dtv_ref[rb, :], la_ref[rb, :],
                                       stb_ref, sele_ref[1], True)
        return carry

    lax.fori_loop(0, nchunk // SSD_UNROLL, body, 0)
    z = z_ref[...]
    y = yf_ref[...] + yb_ref[...] + ux_ref[...] * dsk_ref[...]
    o_ref[...] = y * (z * jax.nn.sigmoid(z))


def ssd(proj, dt, conv_w, conv_b, dt_bias_g, a_neg_g, d_skip_g, bsz, seq):
    m = proj.shape[0]
    gw, n = SSD_GW, SSD_STATE
    f32 = jnp.float32
    kk = SSD_CONV_K
    cb2 = conv_b.reshape(1, -1)
    in_specs = [
        pl.BlockSpec((seq, gw), lambda b, g: (b, g)),
        pl.BlockSpec((seq, gw), lambda b, g: (b, SSD_INNER // gw + g)),
        pl.BlockSpec((seq, n), lambda b, g: (b, 2 * SSD_INNER // n + g)),
        pl.BlockSpec((seq, n), lambda b, g: (b, 2 * SSD_INNER // n + SSD_GROUPS + g)),
        pl.BlockSpec((seq, LANES), lambda b, g: (b, g)),
        pl.BlockSpec((kk, gw), lambda b, g: (0, g)),
        pl.BlockSpec((kk, n), lambda b, g: (0, SSD_INNER // n + g)),
        pl.BlockSpec((kk, n), lambda b, g: (0, SSD_INNER // n + SSD_GROUPS + g)),
        pl.BlockSpec((1, gw), lambda b, g: (0, g)),
        pl.BlockSpec((1, n), lambda b, g: (0, SSD_INNER // n + g)),
        pl.BlockSpec((1, n), lambda b, g: (0, SSD_INNER // n + SSD_GROUPS + g)),
        pl.BlockSpec((1, LANES), lambda b, g: (0, g)),
        pl.BlockSpec((1, LANES), lambda b, g: (0, g)),
        pl.BlockSpec((1, gw), lambda b, g: (0, g)),
        pl.BlockSpec((2, LANES, gw), lambda b, g: (0, 0, 0)),
    ]
    src = jnp.arange(LANES)
    sele = jnp.stack([(src[:, None] == d * SSD_GH + jnp.arange(gw)[None, :] // SSD_HEAD_DIM) for d in range(2)])
    return pl.pallas_call(
        _ssd_kernel,
        grid=(bsz, SSD_GROUPS),
        in_specs=in_specs,
        out_specs=pl.BlockSpec((seq, gw), lambda b, g: (b, g)),
        out_shape=jax.ShapeDtypeStruct((m, SSD_INNER), f32),
        scratch_shapes=[pltpu.VMEM((seq, gw), f32), pltpu.VMEM((seq, n), f32), pltpu.VMEM((seq, n), f32),
                        pltpu.VMEM((seq, LANES), f32), pltpu.VMEM((seq, LANES), f32),
                        pltpu.VMEM((seq, gw), f32), pltpu.VMEM((seq, gw), f32),
                        pltpu.VMEM((n, gw), f32), pltpu.VMEM((n, gw), f32)],
        compiler_params=pltpu.CompilerParams(
            dimension_semantics=("parallel", "parallel"), vmem_limit_bytes=VMEM_LIMIT),
        name="ssd",
    )(proj, proj, proj, proj, dt, conv_w, conv_w, conv_w, cb2, cb2, cb2, dt_bias_g, a_neg_g, d_skip_g,
      sele.astype(jnp.bfloat16))


def _dt_weight(w_dt):
    pad = jnp.zeros((w_dt.shape[0], LANES - 2 * SSD_GH), w_dt.dtype)
    parts = []
    for g in range(SSD_GROUPS):
        parts += [w_dt[:, g * SSD_GH:(g + 1) * SSD_GH],
                  w_dt[:, SSD_HEADS + g * SSD_GH:SSD_HEADS + (g + 1) * SSD_GH], pad]
    return jnp.concatenate(parts, axis=1)


def _group_lanes(v):
    return _dt_weight(v.reshape(1, -1))


def _dt_proj_kernel(x_ref, wh_ref, wl_ref, o_ref):
    hi, lo = _bf16_split(x_ref[...])
    wh = wh_ref[...]
    o_ref[...] = (jnp.dot(hi, wh, preferred_element_type=jnp.float32)
                  + jnp.dot(hi, wl_ref[...], preferred_element_type=jnp.float32)
                  + jnp.dot(lo, wh, preferred_element_type=jnp.float32))


def dt_proj(x, w_dt_g):
    m, k = x.shape
    n = w_dt_g.shape[1]
    wh, wl = _bf16_split(w_dt_g)
    tm = LN_TM
    return pl.pallas_call(
        _dt_proj_kernel,
        grid=(m // tm,),
        in_specs=[pl.BlockSpec((tm, k), lambda i: (i, 0)),
                  pl.BlockSpec((k, n), lambda i: (0, 0)), pl.BlockSpec((k, n), lambda i: (0, 0))],
        out_specs=pl.BlockSpec((tm, n), lambda i: (i, 0)),
        out_shape=jax.ShapeDtypeStruct((m, n), jnp.float32),
        compiler_params=pltpu.CompilerParams(
            dimension_semantics=("parallel",), vmem_limit_bytes=VMEM_LIMIT),
        name="dt_proj",
    )(x, wh, wl)


def _pool_kernel(u_ref, w_ref, sc_ref, o_ref):
    n = u_ref.shape[0]
    row = lax.broadcasted_iota(jnp.int32, (n, POOL_GROUP), 0)
    for gi, win in enumerate(POOL_WINDOWS):
        u = u_ref[:, gi * POOL_GROUP:(gi + 1) * POOL_GROUP]
        half = win // 2
        fwd, bwd = u, u
        s = 1
        while s < half:
            fwd = fwd + _shift_rows(fwd, s, row)
            bwd = bwd + _shift_rows(bwd, -s, row)
            s *= 2
        total = fwd + _shift_rows(bwd, -1, row)
        cnt = (jnp.minimum(row + half, n) - jnp.maximum(row - half, 0)).astype(jnp.float32)
        pooled = total / cnt - u
        mixed = jnp.dot(pooled.astype(jnp.bfloat16), w_ref[gi], preferred_element_type=jnp.float32)
        o_ref[:, gi * POOL_GROUP:(gi + 1) * POOL_GROUP] = mixed * sc_ref[:, gi * POOL_GROUP:(gi + 1) * POOL_GROUP]


def multiscale_pool_kernel(proj, pool_w, pool_scale, bsz, seq):
    m = proj.shape[0]
    return pl.pallas_call(
        _pool_kernel,
        grid=(bsz,),
        in_specs=[pl.BlockSpec((seq, POOL_INNER), lambda b: (b, 0)),
                  pl.BlockSpec(pool_w.shape, lambda b: (0, 0, 0)),
                  pl.BlockSpec((1, POOL_INNER), lambda b: (0, 0))],
        out_specs=pl.BlockSpec((seq, POOL_INNER), lambda b: (b, 0)),
        out_shape=jax.ShapeDtypeStruct((m, POOL_INNER), jnp.float32),
        compiler_params=pltpu.CompilerParams(
            dimension_semantics=("parallel",), vmem_limit_bytes=VMEM_LIMIT),
        name="multiscale_pool",
    )(proj, pool_w.astype(jnp.bfloat16), pool_scale.reshape(1, -1))


def axial_rope_tables(seq):
    rows = seq // GRID_W
    row = jnp.repeat(jnp.arange(rows), GRID_W).astype(jnp.float32)
    col = jnp.tile(jnp.arange(GRID_W), rows).astype(jnp.float32)
    axis_dims = ATT_HEAD_DIM // 2
    freqs = ROPE_THETA ** (-jnp.arange(0, axis_dims, 2, dtype=jnp.float32) / axis_dims)
    ang = jnp.concatenate([row[:, None] * freqs, col[:, None] * freqs], axis=-1)
    return jnp.cos(ang), jnp.sin(ang)


def kernel(x, w_in_ab, ssm_conv_w, ssm_conv_b, ssm_dt_bias, ssm_a_log, ssm_d, ssm_norm_w, attn_q_norm, attn_k_norm, w_out_ab, w_in_cd, pool_w, pool_scale, hgrn_lb_logits, hgrn_norm_w, w_out_cd, router_w, moe_w1, moe_w3, moe_w2, ln1_g, ln1_b, ln2_g, ln2_b):
    bsz, seq, d = x.shape
    m = bsz * seq
    cos_t, sin_t = rope_lane_tables(seq)
    gq = head_mean_matrix()
    lb_all = jnp.cumsum(jax.nn.softmax(hgrn_lb_logits, axis=0), axis=0)
    lb_all = lb_all - lb_all[0]
    x2 = x.reshape(m, d)
    for layer in range(DEPTH):
        j = layer // 2
        if layer % 2 == 0:
            w = w_in_ab[j]
            w_main = jnp.concatenate([w[:, :1536], w[:, 1552:]], axis=1).astype(jnp.bfloat16)
            proj = matmul(x2, w_main)
            dt = dt_proj(x2, _dt_weight(w[:, 1536:1552]))
            ya = ssd(proj, dt, ssm_conv_w[j], ssm_conv_b[j], _group_lanes(ssm_dt_bias[j]),
                     _group_lanes(-jnp.exp(ssm_a_log[j])), jnp.repeat(ssm_d[j], SSD_HEAD_DIM).reshape(1, -1),
                     bsz, seq)
            yb = gqa(proj, cos_t, sin_t, gq, attn_q_norm[j], attn_k_norm[j], bsz, seq)
            w_out, ya_norm_w = w_out_ab[j], ssm_norm_w[j]
        else:
            proj = matmul(x2, w_in_cd[j].astype(jnp.bfloat16))
            ya = multiscale_pool_kernel(proj, pool_w[j], pool_scale[j], bsz, seq)
            yb = hgrn2(proj, lb_all[layer], hgrn_norm_w[j], bsz, seq)
            w_out, ya_norm_w = w_out_cd[j], None
        x2 = moe_block(ya, yb, x2, w_out, ln1_g[layer], ln1_b[layer], router_w[layer],
                       moe_w1, moe_w3, moe_w2, layer, ln2_g[layer], ln2_b[layer], bsz, seq, ya_norm_w)
    return x2.reshape(bsz, seq, d)
```
